```python
import math
import jax, jax.numpy as jnp
from jax import lax
import numpy as np

D_MODEL = 2048
BATCH = 4
SEQ = 2048
DEPTH = 4

HEAD_DIM = 128
D_FF = 5632
MACARON_WEIGHT = 0.5
RMS_EPS = 1e-6
NEG_INF = -1e30

A_HEADS = 8
A_PATTERNS = ((128, 1), (512, 4), (2048, 16))
A_BLOCK = 64
B_HEADS = 8
B_KV_HEADS = 2
B_HALF_WINDOW = 128
B_BLOCK = 128
C_HEADS = 16
GRID_W = 64
C_KR_MAX = 8
C_KC = 16
C_QR_MAX = 8
C_QC = 16

AB_WIDTH = (A_HEADS + B_HEADS) * HEAD_DIM
AB_IN = 3 * A_HEADS * HEAD_DIM + B_HEADS * HEAD_DIM + 2 * B_KV_HEADS * HEAD_DIM
C_WIDTH = C_HEADS * HEAD_DIM
C_IN = 3 * C_WIDTH
N_EVEN = (DEPTH + 1) // 2
N_ODD = DEPTH // 2

kernel_name = "hybrid_dilated_banded_natten_macaron"


def _rmsnorm(x, g):
    x32 = x.astype(jnp.float32)
    y = x32 * lax.rsqrt(jnp.mean(x32 * x32, axis=-1, keepdims=True) + RMS_EPS)
    return (y * g.astype(jnp.float32)).astype(x.dtype)


def _swiglu(h, w_gate, w_up, w_down):
    return (jax.nn.silu(h @ w_gate) * (h @ w_up)) @ w_down


def _alibi_slopes(n):
    return 2.0 ** (-8.0 * jnp.arange(1, n + 1, dtype=jnp.float32) / n)


def _banded_attention(q, k, v, half_window, block, slopes, dist_scale, sink=None):
    L, dh = q.shape[-2], q.shape[-1]
    nb = L // block
    kw = block + 2 * half_window
    pad = [(0, 0)] * (k.ndim - 2) + [(half_window, half_window), (0, 0)]
    kp = jnp.pad(k, pad)
    vp = jnp.pad(v, pad)
    kidx = jnp.arange(nb)[:, None] * block + jnp.arange(kw)[None, :]
    kb = jnp.take(kp, kidx, axis=-2)
    vb = jnp.take(vp, kidx, axis=-2)
    qb = q.reshape(q.shape[:-2] + (nb, block, dh))
    s = jnp.einsum('...hgnqd,...hnkd->...hgnqk', qb, kb).astype(jnp.float32) * (dh ** -0.5)
    qpos = jnp.arange(L).reshape(nb, block)
    kpos = kidx - half_window
    dist = jnp.abs(qpos[:, :, None] - kpos[:, None, :])
    valid = (dist <= half_window) & (kpos >= 0)[:, None, :] & (kpos < L)[:, None, :]
    bias = -slopes[:, :, None, None, None] * (dist * dist_scale).astype(jnp.float32)
    s = jnp.where(valid, s + bias, NEG_INF)
    m = jnp.max(s, axis=-1)
    if sink is not None:
        sink32 = sink.astype(jnp.float32)[:, :, None, None]
        m = jnp.maximum(m, sink32)
    p = jnp.exp(s - m[..., None])
    denom = jnp.sum(p, axis=-1)
    if sink is not None:
        denom = denom + jnp.exp(sink32 - m)
    o = jnp.einsum('...hgnqk,...hnkd->...hgnqd', (p / denom[..., None]).astype(v.dtype), vb)
    lse = m + jnp.log(denom)
    return o.reshape(q.shape), lse.reshape(q.shape[:-1])


def _dilated_mixture(q, k, v, slopes):
    B, S, H, dh = q.shape
    outs, lses = [], []
    for window, r in A_PATTERNS:
        L = S // r
        def to_res(t):
            return t.reshape(B, L, r, H, dh).transpose(0, 2, 3, 1, 4)
        o, lse = _banded_attention(to_res(q)[:, :, :, None], to_res(k), to_res(v),
                                   window // (2 * r), math.gcd(L, A_BLOCK), slopes[:, None], r)
        outs.append(o[:, :, :, 0].transpose(0, 3, 1, 2, 4).reshape(B, S, H, dh))
        lses.append(lse[:, :, :, 0].transpose(0, 3, 1, 2).reshape(B, S, H))
    w = jax.nn.softmax(jnp.stack(lses), axis=0)
    out = jnp.sum(w[..., None] * jnp.stack(outs).astype(jnp.float32), axis=0)
    return out.astype(q.dtype)


def _windowed_gqa_sink(q, k, v, sink):
    B, S, hq, dh = q.shape
    grp = hq // B_KV_HEADS
    qg = q.reshape(B, S, B_KV_HEADS, grp, dh).transpose(0, 2, 3, 1, 4)
    kt = k.transpose(0, 2, 1, 3)
    vt = v.transpose(0, 2, 1, 3)
    slopes = _alibi_slopes(hq).reshape(B_KV_HEADS, grp)
    o, _ = _banded_attention(qg, kt, vt, B_HALF_WINDOW, B_BLOCK, slopes, 1,
                             sink.reshape(B_KV_HEADS, grp))
    return o.transpose(0, 3, 1, 2, 4).reshape(B, S, hq, dh)


def _neighbourhood_attention(q, k, v, rpb):
    B, S, H, dh = q.shape
    rows = S // GRID_W
    kr = min(C_KR_MAX, rows)
    qr = math.gcd(rows, C_QR_MAX)
    span_r = min(rows, kr + qr - 1)
    span_c = min(GRID_W, C_KC + C_QC - 1)
    nrb, ncb = rows // qr, GRID_W // C_QC
    qrow = jnp.arange(rows)
    qcol = jnp.arange(GRID_W)
    row_start = jnp.clip(qrow - kr // 2, 0, rows - kr)
    col_start = jnp.clip(qcol - C_KC // 2, 0, GRID_W - C_KC)
    rb_start = jnp.clip(jnp.arange(nrb) * qr - kr // 2, 0, rows - span_r)
    cb_start = jnp.clip(jnp.arange(ncb) * C_QC - C_KC // 2, 0, GRID_W - span_c)
    kr_idx = rb_start[:, None] + jnp.arange(span_r)[None, :]
    kc_idx = cb_start[:, None] + jnp.arange(span_c)[None, :]

    def gather_kv(t):
        g = jnp.take(t.reshape(B, rows, GRID_W, H, dh), kr_idx, axis=1)
        g = jnp.take(g, kc_idx, axis=3)
        return g.transpose(0, 5, 1, 3, 2, 4, 6).reshape(B, H, nrb, ncb, span_r * span_c, dh)

    kb = gather_kv(k)
    vb = gather_kv(v)
    qb = q.reshape(B, nrb, qr, ncb, C_QC, H, dh).transpose(0, 5, 1, 3, 2, 4, 6)
    qb = qb.reshape(B, H, nrb, ncb, qr * C_QC, dh)
    s = jnp.einsum('bhnmqd,bhnmkd->bhnmqk', qb, kb).astype(jnp.float32) * (dh ** -0.5)

    kr_b = kr_idx[:, None, :]
    rs = row_start.reshape(nrb, qr)[:, :, None]
    in_r = (kr_b >= rs) & (kr_b < rs + kr)
    dr = kr_b - qrow.reshape(nrb, qr)[:, :, None]
    kc_b = kc_idx[:, None, :]
    cs = col_start.reshape(ncb, C_QC)[:, :, None]
    in_c = (kc_b >= cs) & (kc_b < cs + C_KC)
    dc = kc_b - qcol.reshape(ncb, C_QC)[:, :, None]
    ri = jnp.clip(dr + C_KR_MAX - 1, 0, 2 * C_KR_MAX - 2)
    ci = jnp.clip(dc + C_KC - 1, 0, 2 * C_KC - 2)
    bias = rpb.astype(jnp.float32)[:, ri[:, :, :, None, None, None], ci[None, None, None]]
    mask = in_r[:, :, :, None, None, None] & in_c[None, None, None]
    bias = jnp.where(mask[None], bias, NEG_INF)
    bias = bias.transpose(0, 1, 4, 2, 5, 3, 6).reshape(H, nrb, ncb, qr * C_QC, span_r * span_c)
    p = jax.nn.softmax(s + bias[None], axis=-1)
    o = jnp.einsum('bhnmqk,bhnmkd->bhnmqd', p.astype(v.dtype), vb)
    o = o.reshape(B, H, nrb, ncb, qr, C_QC, dh).transpose(0, 2, 4, 3, 5, 1, 6)
    return o.reshape(B, S, H, dh)


def _mixer_ab(h, w_in, w_out, sink):
    B, S, _ = h.shape
    da, db, dkv = A_HEADS * HEAD_DIM, B_HEADS * HEAD_DIM, B_KV_HEADS * HEAD_DIM
    proj = h @ w_in
    qa, ka, va, qb, kb, vb = jnp.split(
        proj, [da, 2 * da, 3 * da, 3 * da + db, 3 * da + db + dkv], axis=-1)
    heads_a = lambda t: t.reshape(B, S, A_HEADS, HEAD_DIM)
    oa = _dilated_mixture(heads_a(qa), heads_a(ka), heads_a(va), _alibi_slopes(A_HEADS))
    ob = _windowed_gqa_sink(qb.reshape(B, S, B_HEADS, HEAD_DIM),
                            kb.reshape(B, S, B_KV_HEADS, HEAD_DIM),
                            vb.reshape(B, S, B_KV_HEADS, HEAD_DIM), sink)
    o = jnp.concatenate([oa.reshape(B, S, da), ob.reshape(B, S, db)], axis=-1)
    return o @ w_out


def _mixer_c(h, w_in, w_out, rpb):
    B, S, _ = h.shape
    q, k, v = jnp.split(h @ w_in, 3, axis=-1)
    heads = lambda t: t.reshape(B, S, C_HEADS, HEAD_DIM)
    o = _neighbourhood_attention(heads(q), heads(k), heads(v), rpb)
    return o.reshape(B, S, C_WIDTH) @ w_out


def setup_inputs(seed: int = 0) -> dict:
    key = jax.random.key(seed)
    ks = jax.random.split(key, 14)
    f32 = jnp.float32
    D, F = D_MODEL, D_FF
    nrm = lambda k, shape: jax.random.normal(k, shape, f32)
    return {
        'x': nrm(ks[0], (BATCH, SEQ, D)),
        'ffn_norm': 1.0 + 0.02 * nrm(ks[1], (DEPTH, 2, D)),
        'ffn_w_gate': nrm(ks[2], (DEPTH, 2, D, F)) * D ** -0.5,
        'ffn_w_up': nrm(ks[3], (DEPTH, 2, D, F)) * D ** -0.5,
        'ffn_w_down': nrm(ks[4], (DEPTH, 2, F, D)) * F ** -0.5,
        'mix_norm': 1.0 + 0.02 * nrm(ks[5], (DEPTH, D)),
        'ab_w_in': nrm(ks[6], (N_EVEN, D, AB_IN)) * D ** -0.5,
        'ab_w_out': nrm(ks[7], (N_EVEN, AB_WIDTH, D)) * AB_WIDTH ** -0.5,
        'ab_sink': 0.5 * nrm(ks[8], (N_EVEN, B_HEADS)),
        'c_w_in': nrm(ks[9], (N_ODD, D, C_IN)) * D ** -0.5,
        'c_w_out': nrm(ks[10], (N_ODD, C_WIDTH, D)) * C_WIDTH ** -0.5,
        'c_rpb': 0.1 * nrm(ks[11], (N_ODD, C_HEADS, 2 * C_KR_MAX - 1, 2 * C_KC - 1)),
        'final_norm': 1.0 + 0.02 * nrm(ks[12], (D,)),
    }


def reference(x, ffn_norm, ffn_w_gate, ffn_w_up, ffn_w_down, mix_norm, ab_w_in, ab_w_out,
              ab_sink, c_w_in, c_w_out, c_rpb, final_norm):
    for layer in range(DEPTH):
        h = _rmsnorm(x, ffn_norm[layer, 0])
        x = x + MACARON_WEIGHT * _swiglu(h, ffn_w_gate[layer, 0], ffn_w_up[layer, 0], ffn_w_down[layer, 0])
        h = _rmsnorm(x, mix_norm[layer])
        i = layer // 2
        if layer % 2 == 0:
            x = x + _mixer_ab(h, ab_w_in[i], ab_w_out[i], ab_sink[i])
        else:
            x = x + _mixer_c(h, c_w_in[i], c_w_out[i], c_rpb[i])
        h = _rmsnorm(x, ffn_norm[layer, 1])
        x = x + MACARON_WEIGHT * _swiglu(h, ffn_w_gate[layer, 1], ffn_w_up[layer, 1], ffn_w_down[layer, 1])
    return _rmsnorm(x, final_norm)
```

```python
import functools
import math

import numpy as np
import jax
import jax.numpy as jnp
from jax import lax
from jax.experimental import pallas as pl
from jax.experimental.pallas import tpu as pltpu

HEAD_DIM = 128
MACARON_WEIGHT = 0.5
RMS_EPS = 1e-6
NEG_INF = -1e30

A_HEADS = 8
A_PATTERNS = ((128, 1), (512, 4), (2048, 16))
B_HEADS = 8
B_KV_HEADS = 2
B_HALF_WINDOW = 128
C_HEADS = 16
GRID_W = 64
C_KR_MAX = 8
C_KC = 16

VMEM_LIMIT_BYTES = 56 * 1024 * 1024

TOKEN_BLOCK = 1024
FFN_HIDDEN_BLOCK = 512
PROJ_COL_BLOCK = 512
ATTN_Q_BLOCK = 256

BF16 = jnp.bfloat16
F32 = jnp.float32


def _rms_normalize(x, g):
    ms = jnp.mean(x * x, axis=-1, keepdims=True)
    return x * lax.rsqrt(ms + RMS_EPS) * g


def _ffn_kernel(x_ref, g_ref, wg_ref, wu_ref, wd_ref, gf_ref, o_ref, h_ref, *, n_hidden_blocks,
                down_col_block, final_norm):
    j = pl.program_id(1)

    @pl.when(j == 0)
    def _():
        h_ref[...] = _rms_normalize(x_ref[...], g_ref[...]).astype(BF16)
        o_ref[...] = jnp.zeros_like(o_ref)

    h = h_ref[...]
    gate = jnp.dot(h, wg_ref[...], preferred_element_type=F32)
    up = jnp.dot(h, wu_ref[...], preferred_element_type=F32)
    act = (gate * jax.nn.sigmoid(gate) * up).astype(BF16)
    d_model = o_ref.shape[1]
    for c0 in range(0, d_model, down_col_block):
        cols = slice(c0, c0 + down_col_block)
        o_ref[:, cols] += jnp.dot(act, wd_ref[:, cols], preferred_element_type=F32)

    @pl.when(j == n_hidden_blocks - 1)
    def _():
        y = x_ref[...] + MACARON_WEIGHT * o_ref[...]
        if final_norm:
            y = _rms_normalize(y, gf_ref[...])
        o_ref[...] = y


def _ffn(x, g, wg, wu, wd, g_final, *, final_norm):
    t, d = x.shape
    f = wg.shape[1]
    tm, tf = min(TOKEN_BLOCK, t), min(FFN_HIDDEN_BLOCK, f)
    assert t % tm == 0 and f % tf == 0
    nj = f // tf
    kern = functools.partial(_ffn_kernel, n_hidden_blocks=nj, down_col_block=min(512, d),
                             final_norm=final_norm)
    return pl.pallas_call(
        kern,
        grid=(t // tm, nj),
        in_specs=[
            pl.BlockSpec((tm, d), lambda i, j: (i, 0), pipeline_mode=pl.Buffered(1)),
            pl.BlockSpec((1, d), lambda i, j: (0, 0)),
            pl.BlockSpec((d, tf), lambda i, j: (0, j)),
            pl.BlockSpec((d, tf), lambda i, j: (0, j)),
            pl.BlockSpec((tf, d), lambda i, j: (j, 0)),
            pl.BlockSpec((1, d), lambda i, j: (0, 0)),
        ],
        out_specs=pl.BlockSpec((tm, d), lambda i, j: (i, 0)),
        out_shape=jax.ShapeDtypeStruct((t, d), F32),
        scratch_shapes=[pltpu.VMEM((tm, d), BF16)],
        compiler_params=pltpu.CompilerParams(
            dimension_semantics=("parallel", "arbitrary"),
            vmem_limit_bytes=VMEM_LIMIT_BYTES),
        name="ffn",
    )(x, g.reshape(1, d), wg, wu, wd, g_final.reshape(1, d))


def _in_proj_kernel(x_ref, g_ref, w_ref, s_ref, o_ref, h_ref):
    @pl.when(pl.program_id(1) == 0)
    def _():
        h_ref[...] = _rms_normalize(x_ref[...], g_ref[...]).astype(BF16)

    acc = jnp.dot(h_ref[...], w_ref[...], preferred_element_type=F32)
    o_ref[...] = (acc * s_ref[...]).astype(o_ref.dtype)


def _in_proj(x, g, w, col_scale):
    t, d = x.shape
    n = w.shape[1]
    tm, tn = min(TOKEN_BLOCK, t), min(PROJ_COL_BLOCK, n)
    assert t % tm == 0 and n % tn == 0
    return pl.pallas_call(
        _in_proj_kernel,
        grid=(t // tm, n // tn),
        in_specs=[
            pl.BlockSpec((tm, d), lambda i, j: (i, 0)),
            pl.BlockSpec((1, d), lambda i, j: (0, 0)),
            pl.BlockSpec((d, tn), lambda i, j: (0, j)),
            pl.BlockSpec((1, tn), lambda i, j: (0, j)),
        ],
        out_specs=pl.BlockSpec((tm, tn), lambda i, j: (i, j)),
        out_shape=jax.ShapeDtypeStruct((t, n), BF16),
        scratch_shapes=[pltpu.VMEM((tm, d), BF16)],
        compiler_params=pltpu.CompilerParams(
            dimension_semantics=("parallel", "arbitrary"),
            vmem_limit_bytes=VMEM_LIMIT_BYTES),
        name="in_proj",
    )(x, g.reshape(1, d), w, col_scale.reshape(1, n))


def _out_proj_kernel(*refs, n_parts):
    o_refs, w_refs = refs[:n_parts], refs[n_parts:2 * n_parts]
    x_ref, out_ref = refs[2 * n_parts], refs[2 * n_parts + 1]
    acc = x_ref[...]
    for o_ref, w_ref in zip(o_refs, w_refs):
        acc = acc + jnp.dot(o_ref[...], w_ref[...], preferred_element_type=F32)
    out_ref[...] = acc


def _out_proj(parts, w, x):
    t, d = x.shape
    tm, tn = min(TOKEN_BLOCK, t), min(PROJ_COL_BLOCK, d)
    widths = [p.shape[1] for p in parts]
    assert sum(widths) == w.shape[0] and len(set(widths)) == 1
    width = widths[0]
    o_specs = [pl.BlockSpec((tm, width), lambda i, j: (i, 0)) for _ in parts]
    w_specs = [pl.BlockSpec((width, tn), functools.partial(lambda i, j, p: (p, j), p=p))
               for p in range(len(parts))]
    return pl.pallas_call(
        functools.partial(_out_proj_kernel, n_parts=len(parts)),
        grid=(t // tm, d // tn),
        in_specs=o_specs + w_specs + [pl.BlockSpec((tm, tn), lambda i, j: (i, j))],
        out_specs=pl.BlockSpec((tm, tn), lambda i, j: (i, j)),
        out_shape=jax.ShapeDtypeStruct((t, d), F32),
        compiler_params=pltpu.CompilerParams(
            dimension_semantics=("parallel", "arbitrary"),
            vmem_limit_bytes=VMEM_LIMIT_BYTES),
        name="out_proj",
    )(*parts, *([w] * len(parts)), x)


def _attn_kernel(*refs, tiles, use_sink):
    if use_sink:
        q_ref, k_ref, v_ref, bias_ref, sink_ref, o_ref = refs
        sink = sink_ref[0, :, 0:1]
    else:
        q_ref, k_ref, v_ref, bias_ref, o_ref = refs
    for q0, bq, k0, kw, case, c0 in tiles:
        q = q_ref[q0:q0 + bq, :]
        k = k_ref[k0:k0 + kw, :]
        s = lax.dot_general(q, k, (((1,), (1,)), ((), ())), preferred_element_type=F32)
        s = s + bias_ref[0, case, :, c0:c0 + kw]
        m = jnp.max(s, axis=-1, keepdims=True)
        if use_sink:
            m = jnp.maximum(m, sink)
        p = jnp.exp(s - m)
        denom = jnp.sum(p, axis=-1, keepdims=True)
        if use_sink:
            denom = denom + jnp.exp(sink - m)
        o = jnp.dot(p.astype(BF16), v_ref[k0:k0 + kw, :], preferred_element_type=F32)
        o_ref[q0:q0 + bq, :] = (o / denom).astype(o_ref.dtype)


def _attention(qkv, bias, sink, *, batch, seq, n_heads, group, q_col, k_col, v_col, tiles):
    dh = HEAD_DIM
    use_sink = sink is not None
    in_specs = [
        pl.BlockSpec((seq, dh), lambda h, b: (b, q_col + h)),
        pl.BlockSpec((seq, dh), lambda h, b: (b, k_col + h // group)),
        pl.BlockSpec((seq, dh), lambda h, b: (b, v_col + h // group)),
        pl.BlockSpec((1,) + bias.shape[1:], lambda h, b: (h, 0, 0, 0)),
    ]
    args = [qkv, qkv, qkv, bias]
    if use_sink:
        in_specs.append(pl.BlockSpec((1, 1, dh), lambda h, b: (h, 0, 0)))
        args.append(jnp.broadcast_to(sink.astype(F32)[:, None, None], (n_heads, 1, dh)))
    return pl.pallas_call(
        functools.partial(_attn_kernel, tiles=tuple(tiles), use_sink=use_sink),
        grid=(n_heads, batch),
        in_specs=in_specs,
        out_specs=pl.BlockSpec((seq, dh), lambda h, b: (b, h)),
        out_shape=jax.ShapeDtypeStruct((batch * seq, n_heads * dh), BF16),
        compiler_params=pltpu.CompilerParams(
            dimension_semantics=("parallel", "arbitrary"),
            vmem_limit_bytes=VMEM_LIMIT_BYTES),
        name="attention",
    )(*args)


def _alibi_slopes(n):
    return 2.0 ** (-8.0 * jnp.arange(1, n + 1, dtype=F32) / n)


def _toeplitz_tiles(seq, bq, halo):
    tiles = []
    for q0 in range(0, seq, bq):
        k0, k1 = max(0, q0 - halo), min(seq, q0 + bq + halo)
        tiles.append((q0, bq, k0, k1 - k0, 0, k0 - q0 + halo))
    return tiles


def _toeplitz_offsets(bq, halo):
    return (jnp.arange(bq + 2 * halo, dtype=jnp.int32)[None, :] - halo
            - jnp.arange(bq, dtype=jnp.int32)[:, None])


def _dilated_bias(bq, halo):
    d = _toeplitz_offsets(bq, halo)
    ad = jnp.abs(d)
    count = jnp.zeros(d.shape, F32)
    for window, r in A_PATTERNS:
        count = count + ((d % r == 0) & (ad <= (window // (2 * r)) * r)).astype(F32)
    log_count = jnp.where(count > 0, jnp.log(jnp.maximum(count, 1.0)), NEG_INF)
    slopes = _alibi_slopes(A_HEADS)
    return (log_count[None] - slopes[:, None, None] * ad.astype(F32)[None])[:, None]


def _banded_bias(bq, halo):
    d = _toeplitz_offsets(bq, halo)
    ad = jnp.abs(d)
    slopes = _alibi_slopes(B_HEADS)
    bias = jnp.where((ad <= B_HALF_WINDOW)[None], -slopes[:, None, None] * ad.astype(F32)[None], NEG_INF)
    return bias[:, None]


def _neighbourhood_geometry(seq, q_rows, k_rows):
    rows = seq // GRID_W
    kr = min(C_KR_MAX, rows)
    k_rows = min(k_rows, rows)
    assert rows % q_rows == 0 and k_rows >= min(rows, q_rows + kr - 1)
    cases, tiles = {}, []
    for r0 in range(0, rows, q_rows):
        kstart = int(np.clip(r0 - kr // 2, 0, rows - k_rows))
        qr = r0 + np.arange(q_rows * GRID_W) // GRID_W
        qc = np.arange(q_rows * GRID_W) % GRID_W
        krow = kstart + np.arange(k_rows * GRID_W) // GRID_W
        kcol = np.arange(k_rows * GRID_W) % GRID_W
        rs = np.clip(qr - kr // 2, 0, rows - kr)[:, None]
        cs = np.clip(qc - C_KC // 2, 0, GRID_W - C_KC)[:, None]
        in_r = (krow[None] >= rs) & (krow[None] < rs + kr)
        in_c = (kcol[None] >= cs) & (kcol[None] < cs + C_KC)
        ri = np.clip(krow[None] - qr[:, None] + C_KR_MAX - 1, 0, 2 * C_KR_MAX - 2)
        ci = np.clip(kcol[None] - qc[:, None] + C_KC - 1, 0, 2 * C_KC - 2)
        key = (r0 - kstart, (in_r & in_c).tobytes())
        if key not in cases:
            cases[key] = (len(cases), ri, ci, in_r & in_c)
        tiles.append((r0 * GRID_W, q_rows * GRID_W, kstart * GRID_W, k_rows * GRID_W, cases[key][0], 0))
    ordered = sorted(cases.values(), key=lambda c: c[0])
    return tiles, ordered


def _neighbourhood_bias(rpb, cases):
    tables = [jnp.where(mask[None], rpb.astype(F32)[:, ri, ci], NEG_INF) for _, ri, ci, mask in cases]
    return jnp.stack(tables, axis=1)


def kernel(x, ffn_norm, ffn_w_gate, ffn_w_up, ffn_w_down, mix_norm, ab_w_in, ab_w_out, ab_sink,
           c_w_in, c_w_out, c_rpb, final_norm):
    batch, seq, d = x.shape
    depth = ffn_norm.shape[0]
    dh = HEAD_DIM
    scale = dh ** -0.5

    wg, wu, wd = ffn_w_gate.astype(BF16), ffn_w_up.astype(BF16), ffn_w_down.astype(BF16)
    ab_in, ab_out = ab_w_in.astype(BF16), ab_w_out.astype(BF16)
    c_in, c_out = c_w_in.astype(BF16), c_w_out.astype(BF16)

    da, db, dkv = A_HEADS * dh, B_HEADS * dh, B_KV_HEADS * dh
    ab_scale = jnp.concatenate([jnp.full((da,), scale, F32), jnp.ones((2 * da,), F32),
                                jnp.full((db,), scale, F32), jnp.ones((2 * dkv,), F32)])
    dc = C_HEADS * dh
    c_scale = jnp.concatenate([jnp.full((dc,), scale, F32), jnp.ones((2 * dc,), F32)])

    bq = min(ATTN_Q_BLOCK, seq)
    halo_a = max((w // (2 * r)) * r for w, r in A_PATTERNS)
    tiles_a, bias_a = _toeplitz_tiles(seq, bq, halo_a), _dilated_bias(bq, halo_a)
    tiles_b, bias_b = _toeplitz_tiles(seq, bq, B_HALF_WINDOW), _banded_bias(bq, B_HALF_WINDOW)
    rows = seq // GRID_W
    q_rows = math.gcd(rows, 4)
    tiles_c, cases_c = _neighbourhood_geometry(seq, q_rows, q_rows + C_KR_MAX)

    xf = x.reshape(batch * seq, d)
    for layer in range(depth):
        i = layer // 2
        xf = _ffn(xf, ffn_norm[layer, 0], wg[layer, 0], wu[layer, 0], wd[layer, 0], final_norm,
                  final_norm=False)
        if layer % 2 == 0:
            proj = _in_proj(xf, mix_norm[layer], ab_in[i], ab_scale)
            oa = _attention(proj, bias_a, None, batch=batch, seq=seq, n_heads=A_HEADS, group=1,
                            q_col=0, k_col=A_HEADS, v_col=2 * A_HEADS, tiles=tiles_a)
            ob = _attention(proj, bias_b, ab_sink[i], batch=batch, seq=seq, n_heads=B_HEADS,
                            group=B_HEADS // B_KV_HEADS, q_col=3 * A_HEADS,
                            k_col=3 * A_HEADS + B_HEADS, v_col=3 * A_HEADS + B_HEADS + B_KV_HEADS,
                            tiles=tiles_b)
            xf = _out_proj([oa, ob], ab_out[i], xf)
        else:
            proj = _in_proj(xf, mix_norm[layer], c_in[i], c_scale)
            bias_c = _neighbourhood_bias(c_rpb[i], cases_c)
            oc = _attention(proj, bias_c, None, batch=batch, seq=seq, n_heads=C_HEADS, group=1,
                            q_col=0, k_col=C_HEADS, v_col=2 * C_HEADS, tiles=tiles_c)
            xf = _out_proj([oc], c_out[i], xf)
        xf = _ffn(xf, ffn_norm[layer, 1], wg[layer, 1], wu[layer, 1], wd[layer, 1], final_norm,
                  final_norm=(layer == depth - 1))
    return xf.reshape(batch, seq, d)
```

```python
import functools
import math

import numpy as np
import jax
import jax.numpy as jnp
from jax import lax
from jax.experimental import pallas as pl
from jax.experimental.pallas import tpu as pltpu

HEAD_DIM = 128
MACARON_WEIGHT = 0.5
RMS_EPS = 1e-6
NEG_INF = -1e30

A_HEADS = 8
A_PATTERNS = ((128, 1), (512, 4), (2048, 16))
B_HEADS = 8
B_KV_HEADS = 2
B_HALF_WINDOW = 128
C_HEADS = 16
GRID_W = 64
C_KR_MAX = 8
C_KC = 16

VMEM_LIMIT_BYTES = 56 * 1024 * 1024

TOKEN_BLOCK = 1024
FFN_HIDDEN_BLOCK = 512
PROJ_COL_BLOCK = 512
ATTN_Q_BLOCK = 256
NEIGHBOURHOOD_Q_ROWS = 4

BF16 = jnp.bfloat16
F32 = jnp.float32


def _rms_normalize(x, g):
    ms = jnp.mean(x * x, axis=-1, keepdims=True)
    return x * lax.rsqrt(ms + RMS_EPS) * g


def _ffn_kernel(x_ref, g_ref, wg_ref, wu_ref, wd_ref, gf_ref, o_ref, h_ref, *, n_hidden_blocks,
                down_col_block, final_norm):
    j = pl.program_id(1)

    @pl.when(j == 0)
    def _():
        x = x_ref[...]
        h_ref[...] = _rms_normalize(x, g_ref[...]).astype(BF16)
        o_ref[...] = x

    h = h_ref[...]
    gate = jnp.dot(h, wg_ref[...], preferred_element_type=F32)
    up = jnp.dot(h, wu_ref[...], preferred_element_type=F32)
    act = (MACARON_WEIGHT * (gate * jax.nn.sigmoid(gate)) * up).astype(BF16)
    d_model = o_ref.shape[1]
    for c0 in range(0, d_model, down_col_block):
        cols = slice(c0, c0 + down_col_block)
        o_ref[:, cols] += jnp.dot(act, wd_ref[:, cols], preferred_element_type=F32)

    if final_norm:
        @pl.when(j == n_hidden_blocks - 1)
        def _():
            o_ref[...] = _rms_normalize(o_ref[...], gf_ref[...])


def _ffn(x, g, wg, wu, wd, g_final, *, layer, half, final_norm):
    t, d = x.shape
    f = wg.shape[-1]
    tm, tf = min(TOKEN_BLOCK, t), min(FFN_HIDDEN_BLOCK, f)
    assert t % tm == 0 and f % tf == 0
    nj = f // tf
    kern = functools.partial(_ffn_kernel, n_hidden_blocks=nj, down_col_block=min(512, d),
                             final_norm=final_norm)
    return pl.pallas_call(
        kern,
        grid=(t // tm, nj),
        in_specs=[
            pl.BlockSpec((tm, d), lambda i, j: (i, 0), pipeline_mode=pl.Buffered(1)),
            pl.BlockSpec((None, None, 1, d), lambda i, j: (layer, half, 0, 0)),
            pl.BlockSpec((None, None, d, tf), lambda i, j: (layer, half, 0, j)),
            pl.BlockSpec((None, None, d, tf), lambda i, j: (layer, half, 0, j)),
            pl.BlockSpec((None, None, tf, d), lambda i, j: (layer, half, j, 0)),
            pl.BlockSpec((1, d), lambda i, j: (0, 0)),
        ],
        out_specs=pl.BlockSpec((tm, d), lambda i, j: (i, 0)),
        out_shape=jax.ShapeDtypeStruct((t, d), F32),
        scratch_shapes=[pltpu.VMEM((tm, d), BF16)],
        compiler_params=pltpu.CompilerParams(
            dimension_semantics=("parallel", "arbitrary"),
            vmem_limit_bytes=VMEM_LIMIT_BYTES),
        name="ffn",
    )(x, g.reshape(g.shape[0], 2, 1, d), wg, wu, wd, g_final.reshape(1, d))


def _in_proj_kernel(x_ref, g_ref, w_ref, s_ref, o_ref, h_ref):
    @pl.when(pl.program_id(1) == 0)
    def _():
        h_ref[...] = _rms_normalize(x_ref[...], g_ref[...]).astype(BF16)

    acc = jnp.dot(h_ref[...], w_ref[...], preferred_element_type=F32)
    o_ref[...] = (acc * s_ref[...]).astype(o_ref.dtype)


def _in_proj(x, g, w, col_scale, *, layer, index):
    t, d = x.shape
    n = w.shape[-1]
    tm, tn = min(TOKEN_BLOCK, t), min(PROJ_COL_BLOCK, n)
    assert t % tm == 0 and n % tn == 0
    return pl.pallas_call(
        _in_proj_kernel,
        grid=(t // tm, n // tn),
        in_specs=[
            pl.BlockSpec((tm, d), lambda i, j: (i, 0)),
            pl.BlockSpec((None, 1, d), lambda i, j: (layer, 0, 0)),
            pl.BlockSpec((None, d, tn), lambda i, j: (index, 0, j)),
            pl.BlockSpec((1, tn), lambda i, j: (0, j)),
        ],
        out_specs=pl.BlockSpec((tm, tn), lambda i, j: (i, j)),
        out_shape=jax.ShapeDtypeStruct((t, n), BF16),
        scratch_shapes=[pltpu.VMEM((tm, d), BF16)],
        compiler_params=pltpu.CompilerParams(
            dimension_semantics=("parallel", "arbitrary"),
            vmem_limit_bytes=VMEM_LIMIT_BYTES),
        name="in_proj",
    )(x, g.reshape(g.shape[0], 1, d), w, col_scale.reshape(1, n))


def _out_proj_kernel(*refs, n_parts):
    o_refs, w_refs = refs[:n_parts], refs[n_parts:2 * n_parts]
    x_ref, out_ref = refs[2 * n_parts], refs[2 * n_parts + 1]
    acc = x_ref[...]
    for o_ref, w_ref in zip(o_refs, w_refs):
        acc = acc + jnp.dot(o_ref[...], w_ref[...], preferred_element_type=F32)
    out_ref[...] = acc


def _out_proj(parts, w, x, *, index):
    t, d = x.shape
    tm, tn = min(TOKEN_BLOCK, t), min(PROJ_COL_BLOCK, d)
    widths = [p.shape[1] for p in parts]
    assert sum(widths) == w.shape[1] and len(set(widths)) == 1
    width = widths[0]
    o_specs = [pl.BlockSpec((tm, width), lambda i, j: (i, 0)) for _ in parts]
    w_specs = [pl.BlockSpec((None, width, tn), functools.partial(lambda i, j, p: (index, p, j), p=p))
               for p in range(len(parts))]
    return pl.pallas_call(
        functools.partial(_out_proj_kernel, n_parts=len(parts)),
        grid=(t // tm, d // tn),
        in_specs=o_specs + w_specs + [pl.BlockSpec((tm, tn), lambda i, j: (i, j))],
        out_specs=pl.BlockSpec((tm, tn), lambda i, j: (i, j)),
        out_shape=jax.ShapeDtypeStruct((t, d), F32),
        compiler_params=pltpu.CompilerParams(
            dimension_semantics=("parallel", "arbitrary"),
            vmem_limit_bytes=VMEM_LIMIT_BYTES),
        name="out_proj",
    )(*parts, *([w] * len(parts)), x)


def _attn_kernel(*refs, tiles, use_sink):
    if use_sink:
        q_ref, k_ref, v_ref, bias_ref, sink_ref, o_ref = refs
        sink = sink_ref[0, :, 0:1]
    else:
        q_ref, k_ref, v_ref, bias_ref, o_ref = refs
    for q0, bq, k0, kw, case, c0 in tiles:
        q = q_ref[q0:q0 + bq, :]
        k = k_ref[k0:k0 + kw, :]
        s = lax.dot_general(q, k, (((1,), (1,)), ((), ())), preferred_element_type=F32)
        s = s + bias_ref[0, case, :, c0:c0 + kw]
        m = jnp.max(s, axis=-1, keepdims=True)
        if use_sink:
            m = jnp.maximum(m, sink)
        p = jnp.exp(s - m)
        denom = jnp.sum(p, axis=-1, keepdims=True)
        if use_sink:
            denom = denom + jnp.exp(sink - m)
        o = jnp.dot(p.astype(BF16), v_ref[k0:k0 + kw, :], preferred_element_type=F32)
        o_ref[q0:q0 + bq, :] = (o / denom).astype(o_ref.dtype)


def _attention(qkv, bias, sink, *, batch, seq, n_heads, group, q_col, k_col, v_col, tiles):
    dh = HEAD_DIM
    use_sink = sink is not None
    in_specs = [
        pl.BlockSpec((seq, dh), lambda h, b: (b, q_col + h)),
        pl.BlockSpec((seq, dh), lambda h, b: (b, k_col + h // group)),
        pl.BlockSpec((seq, dh), lambda h, b: (b, v_col + h // group)),
        pl.BlockSpec((1,) + bias.shape[1:], lambda h, b: (h, 0, 0, 0)),
    ]
    args = [qkv, qkv, qkv, bias]
    if use_sink:
        in_specs.append(pl.BlockSpec((1, 1, dh), lambda h, b: (h, 0, 0)))
        args.append(jnp.broadcast_to(sink.astype(F32)[:, None, None], (n_heads, 1, dh)))
    return pl.pallas_call(
        functools.partial(_attn_kernel, tiles=tuple(tiles), use_sink=use_sink),
        grid=(n_heads, batch),
        in_specs=in_specs,
        out_specs=pl.BlockSpec((seq, dh), lambda h, b: (b, h)),
        out_shape=jax.ShapeDtypeStruct((batch * seq, n_heads * dh), BF16),
        compiler_params=pltpu.CompilerParams(
            dimension_semantics=("parallel", "arbitrary"),
            vmem_limit_bytes=VMEM_LIMIT_BYTES),
        name="attention",
    )(*args)


def _alibi_slopes(n):
    return 2.0 ** (-8.0 * jnp.arange(1, n + 1, dtype=F32) / n)


def _toeplitz_tiles(seq, bq, halo):
    tiles = []
    for q0 in range(0, seq, bq):
        k0, k1 = max(0, q0 - halo), min(seq, q0 + bq + halo)
        tiles.append((q0, bq, k0, k1 - k0, 0, k0 - q0 + halo))
    return tiles


def _toeplitz_offsets(bq, halo):
    return (jnp.arange(bq + 2 * halo, dtype=jnp.int32)[None, :] - halo
            - jnp.arange(bq, dtype=jnp.int32)[:, None])


def _dilated_bias(bq, halo):
    d = _toeplitz_offsets(bq, halo)
    ad = jnp.abs(d)
    count = jnp.zeros(d.shape, F32)
    for window, r in A_PATTERNS:
        count = count + ((d % r == 0) & (ad <= (window // (2 * r)) * r)).astype(F32)
    log_count = jnp.where(count > 0, jnp.log(jnp.maximum(count, 1.0)), NEG_INF)
    slopes = _alibi_slopes(A_HEADS)
    return (log_count[None] - slopes[:, None, None] * ad.astype(F32)[None])[:, None]


def _banded_bias(bq, halo):
    d = _toeplitz_offsets(bq, halo)
    ad = jnp.abs(d)
    slopes = _alibi_slopes(B_HEADS)
    bias = jnp.where((ad <= B_HALF_WINDOW)[None], -slopes[:, None, None] * ad.astype(F32)[None], NEG_INF)
    return bias[:, None]


def _neighbourhood_geometry(seq, q_rows, k_rows):
    rows = seq // GRID_W
    kr = min(C_KR_MAX, rows)
    k_rows = min(k_rows, rows)
    assert rows % q_rows == 0 and k_rows >= min(rows, q_rows + kr - 1)
    qc, kc = np.arange(GRID_W)[:, None], np.arange(GRID_W)[None, :]
    cs = np.clip(qc - C_KC // 2, 0, GRID_W - C_KC)
    in_c = (kc >= cs) & (kc < cs + C_KC)
    ci = np.clip(kc - qc + C_KC - 1, 0, 2 * C_KC - 2)
    cases, tiles = {}, []
    for r0 in range(0, rows, q_rows):
        kstart = int(np.clip(r0 - kr // 2, 0, rows - k_rows))
        qr = r0 + np.arange(q_rows)[:, None]
        krow = kstart + np.arange(k_rows)[None, :]
        rs = np.clip(qr - kr // 2, 0, rows - kr)
        in_r = (krow >= rs) & (krow < rs + kr)
        ri = np.clip(krow - qr + C_KR_MAX - 1, 0, 2 * C_KR_MAX - 2)
        key = (ri.tobytes(), in_r.tobytes())
        if key not in cases:
            cases[key] = (len(cases), ri, in_r)
        tiles.append((r0 * GRID_W, q_rows * GRID_W, kstart * GRID_W, k_rows * GRID_W, cases[key][0], 0))
    ordered = sorted(cases.values(), key=lambda c: c[0])
    row_sel = np.stack([ri[..., None] == np.arange(2 * C_KR_MAX - 1) for _, ri, _ in ordered])
    col_sel = ci[..., None] == np.arange(2 * C_KC - 1)
    mask = np.stack([in_r[:, None, :, None] & in_c[None, :, None, :] for _, _, in_r in ordered])
    return tiles, (row_sel.astype(np.float32), col_sel.astype(np.float32), mask)


def _neighbourhood_bias(rpb, geometry):
    row_sel, col_sel, mask = geometry
    hp = lax.Precision.HIGHEST
    t = jnp.einsum('hij,cqki->hcqkj', rpb.astype(F32), row_sel, precision=hp)
    t = jnp.einsum('hcqkj,xyj->hcqxky', t, col_sel, precision=hp)
    t = jnp.where(mask[None], t, NEG_INF)
    n_cases, q_rows, w, k_rows, _ = mask.shape
    return t.reshape(rpb.shape[0], n_cases, q_rows * w, k_rows * w)


def kernel(x, ffn_norm, ffn_w_gate, ffn_w_up, ffn_w_down, mix_norm, ab_w_in, ab_w_out, ab_sink,
           c_w_in, c_w_out, c_rpb, final_norm):
    batch, seq, d = x.shape
    depth = ffn_norm.shape[0]
    dh = HEAD_DIM
    scale = dh ** -0.5

    wg, wu, wd = ffn_w_gate.astype(BF16), ffn_w_up.astype(BF16), ffn_w_down.astype(BF16)
    ab_in, ab_out = ab_w_in.astype(BF16), ab_w_out.astype(BF16)
    c_in, c_out = c_w_in.astype(BF16), c_w_out.astype(BF16)

    da, db, dkv = A_HEADS * dh, B_HEADS * dh, B_KV_HEADS * dh
    ab_scale = jnp.concatenate([jnp.full((da,), scale, F32), jnp.ones((2 * da,), F32),
                                jnp.full((db,), scale, F32), jnp.ones((2 * dkv,), F32)])
    dc = C_HEADS * dh
    c_scale = jnp.concatenate([jnp.full((dc,), scale, F32), jnp.ones((2 * dc,), F32)])

    bq = min(ATTN_Q_BLOCK, seq)
    halo_a = max((w // (2 * r)) * r for w, r in A_PATTERNS)
    tiles_a, bias_a = _toeplitz_tiles(seq, bq, halo_a), _dilated_bias(bq, halo_a)
    tiles_b, bias_b = _toeplitz_tiles(seq, bq, B_HALF_WINDOW), _banded_bias(bq, B_HALF_WINDOW)
    rows = seq // GRID_W
    q_rows = math.gcd(rows, NEIGHBOURHOOD_Q_ROWS)
    tiles_c, geometry_c = _neighbourhood_geometry(seq, q_rows, q_rows + C_KR_MAX)

    xf = x.reshape(batch * seq, d)
    for layer in range(depth):
        i = layer // 2
        xf = _ffn(xf, ffn_norm, wg, wu, wd, final_norm, layer=layer, half=0, final_norm=False)
        if layer % 2 == 0:
            proj = _in_proj(xf, mix_norm, ab_in, ab_scale, layer=layer, index=i)
            oa = _attention(proj, bias_a, None, batch=batch, seq=seq, n_heads=A_HEADS, group=1,
                            q_col=0, k_col=A_HEADS, v_col=2 * A_HEADS, tiles=tiles_a)
            ob = _attention(proj, bias_b, ab_sink[i], batch=batch, seq=seq, n_heads=B_HEADS,
                            group=B_HEADS // B_KV_HEADS, q_col=3 * A_HEADS,
                            k_col=3 * A_HEADS + B_HEADS, v_col=3 * A_HEADS + B_HEADS + B_KV_HEADS,
                            tiles=tiles_b)
            xf = _out_proj([oa, ob], ab_out, xf, index=i)
        else:
            proj = _in_proj(xf, mix_norm, c_in, c_scale, layer=layer, index=i)
            bias_c = _neighbourhood_bias(c_rpb[i], geometry_c)
            oc = _attention(proj, bias_c, None, batch=batch, seq=seq, n_heads=C_HEADS, group=1,
                            q_col=0, k_col=C_HEADS, v_col=2 * C_HEADS, tiles=tiles_c)
            xf = _out_proj([oc], c_out, xf, index=i)
        xf = _ffn(xf, ffn_norm, wg, wu, wd, final_norm, layer=layer, half=1,
                  final_norm=(layer == depth - 1))
    return xf.reshape(batch, seq, d)
```

```python
import collections
import functools
import math

import numpy as np
import jax
import jax.numpy as jnp
from jax import lax
from jax.experimental import pallas as pl
from jax.experimental.pallas import tpu as pltpu

HEAD_DIM = 128
MACARON_WEIGHT = 0.5
RMS_EPS = 1e-6
NEG_INF = -1e30

A_HEADS = 8
A_PATTERNS = ((128, 1), (512, 4), (2048, 16))
B_HEADS = 8
B_KV_HEADS = 2
B_HALF_WINDOW = 128
C_HEADS = 16
GRID_W = 64
C_KR_MAX = 8
C_KC = 16

LANES = 128
VMEM_LIMIT_BYTES = 56 * 1024 * 1024

TOKEN_BLOCK = 1024
FFN_HIDDEN_BLOCK = 512
IN_PROJ_COL_BLOCK = 1536
OUT_PROJ_COL_BLOCK = 1024
ATTN_Q_BLOCK = 256
NEIGHBOURHOOD_Q_ROWS = 4

BF16 = jnp.bfloat16
F32 = jnp.float32

Weight = collections.namedtuple("Weight", "arr prefix")


def _weight_shape(w):
    return w.arr.shape[len(w.prefix):]


def _weight_spec(w, block, index_fn):
    prefix = tuple(w.prefix)
    return pl.BlockSpec((None,) * len(prefix) + tuple(block), lambda i, j: prefix + tuple(index_fn(i, j)))


def _col_block(n, cap):
    best = max(c for c in range(LANES, min(n, cap) + 1, LANES) if n % c == 0)
    return best


def _rms_normalize(x, g):
    ms = jnp.mean(x * x, axis=-1, keepdims=True)
    return x * lax.rsqrt(ms + RMS_EPS) * g


def _ffn_kernel(*refs, n_hidden_blocks, down_col_block, final_norm, n_casts):
    x_ref, g_ref, wg_ref, wu_ref, wd_ref, gf_ref = refs[:6]
    src_refs = refs[6:6 + n_casts]
    o_ref = refs[6 + n_casts]
    dst_refs = refs[7 + n_casts:7 + 2 * n_casts]
    h_ref = refs[7 + 2 * n_casts]
    j = pl.program_id(1)

    @pl.when(j == 0)
    def _():
        x = x_ref[...]
        h_ref[...] = _rms_normalize(x, g_ref[...]).astype(BF16)
        o_ref[...] = x

    h = h_ref[...]
    gate = jnp.dot(h, wg_ref[...], preferred_element_type=F32)
    up = jnp.dot(h, wu_ref[...], preferred_element_type=F32)
    act = (MACARON_WEIGHT * (gate * jax.nn.sigmoid(gate)) * up).astype(BF16)
    d_model = o_ref.shape[1]
    for c0 in range(0, d_model, down_col_block):
        cols = slice(c0, c0 + down_col_block)
        o_ref[:, cols] += jnp.dot(act, wd_ref[:, cols], preferred_element_type=F32)

    for src_ref, dst_ref in zip(src_refs, dst_refs):
        dst_ref[...] = src_ref[...].astype(BF16)

    if final_norm:
        @pl.when(j == n_hidden_blocks - 1)
        def _():
            o_ref[...] = _rms_normalize(o_ref[...], gf_ref[...])


def _ffn(x, g, wg, wu, wd, g_final, casts, *, final_norm):
    t, d = x.shape
    f = _weight_shape(wg)[1]
    tm, tf = min(TOKEN_BLOCK, t), min(FFN_HIDDEN_BLOCK, f)
    assert t % tm == 0 and f % tf == 0
    ni, nj = t // tm, f // tf

    cast_in, cast_out, cast_shapes = [], [], []
    for w in casts:
        r, c = _weight_shape(w)
        assert r % (16 * ni) == 0
        rb = r // ni
        cb = min(cb for cb in range(LANES, c + 1, LANES) if c % cb == 0 and c // cb <= nj)
        last = c // cb - 1
        index_fn = functools.partial(lambda i, j, last: (i, jnp.minimum(j, last)), last=last)
        cast_in.append(_weight_spec(w, (rb, cb), index_fn))
        cast_out.append(pl.BlockSpec((rb, cb), index_fn))
        cast_shapes.append(jax.ShapeDtypeStruct((r, c), BF16))

    kern = functools.partial(_ffn_kernel, n_hidden_blocks=nj, down_col_block=min(512, d),
                             final_norm=final_norm, n_casts=len(casts))
    outs = pl.pallas_call(
        kern,
        grid=(ni, nj),
        in_specs=[
            pl.BlockSpec((tm, d), lambda i, j: (i, 0), pipeline_mode=pl.Buffered(1)),
            _weight_spec(g, (1, d), lambda i, j: (0, 0)),
            _weight_spec(wg, (d, tf), lambda i, j: (0, j)),
            _weight_spec(wu, (d, tf), lambda i, j: (0, j)),
            _weight_spec(wd, (tf, d), lambda i, j: (j, 0)),
            pl.BlockSpec((1, d), lambda i, j: (0, 0)),
        ] + cast_in,
        out_specs=[pl.BlockSpec((tm, d), lambda i, j: (i, 0))] + cast_out,
        out_shape=[jax.ShapeDtypeStruct((t, d), F32)] + cast_shapes,
        scratch_shapes=[pltpu.VMEM((tm, d), BF16)],
        compiler_params=pltpu.CompilerParams(
            dimension_semantics=("parallel", "arbitrary"),
            vmem_limit_bytes=VMEM_LIMIT_BYTES),
        name="ffn",
    )(x, g.arr, wg.arr, wu.arr, wd.arr, g_final.reshape(1, d), *[w.arr for w in casts])
    return outs[0], list(outs[1:])


def _in_proj_kernel(x_ref, g_ref, w_ref, s_ref, o_ref, h_ref):
    @pl.when(pl.program_id(1) == 0)
    def _():
        h_ref[...] = _rms_normalize(x_ref[...], g_ref[...]).astype(BF16)

    acc = jnp.dot(h_ref[...], w_ref[...], preferred_element_type=F32)
    o_ref[...] = (acc * s_ref[...]).astype(o_ref.dtype)


def _in_proj(x, g, w, col_scale):
    t, d = x.shape
    n = w.shape[1]
    tm, tn = min(TOKEN_BLOCK, t), _col_block(n, IN_PROJ_COL_BLOCK)
    assert t % tm == 0
    return pl.pallas_call(
        _in_proj_kernel,
        grid=(t // tm, n // tn),
        in_specs=[
            pl.BlockSpec((tm, d), lambda i, j: (i, 0)),
            _weight_spec(g, (1, d), lambda i, j: (0, 0)),
            pl.BlockSpec((d, tn), lambda i, j: (0, j)),
            pl.BlockSpec((1, tn), lambda i, j: (0, j)),
        ],
        out_specs=pl.BlockSpec((tm, tn), lambda i, j: (i, j)),
        out_shape=jax.ShapeDtypeStruct((t, n), BF16),
        scratch_shapes=[pltpu.VMEM((tm, d), BF16)],
        compiler_params=pltpu.CompilerParams(
            dimension_semantics=("parallel", "arbitrary"),
            vmem_limit_bytes=VMEM_LIMIT_BYTES),
        name="in_proj",
    )(x, g.arr, w, col_scale.reshape(1, n))


def _out_proj_kernel(*refs, n_parts):
    o_refs, w_refs = refs[:n_parts], refs[n_parts:2 * n_parts]
    x_ref, out_ref = refs[2 * n_parts], refs[2 * n_parts + 1]
    acc = x_ref[...]
    for o_ref, w_ref in zip(o_refs, w_refs):
        acc = acc + jnp.dot(o_ref[...], w_ref[...], preferred_element_type=F32)
    out_ref[...] = acc


def _out_proj(parts, w, x):
    t, d = x.shape
    tm, tn = min(TOKEN_BLOCK, t), _col_block(d, OUT_PROJ_COL_BLOCK)
    widths = [p.shape[1] for p in parts]
    assert sum(widths) == w.shape[0] and len(set(widths)) == 1
    width = widths[0]
    o_specs = [pl.BlockSpec((tm, width), lambda i, j: (i, 0)) for _ in parts]
    w_specs = [pl.BlockSpec((width, tn), functools.partial(lambda i, j, p: (p, j), p=p))
               for p in range(len(parts))]
    return pl.pallas_call(
        functools.partial(_out_proj_kernel, n_parts=len(parts)),
        grid=(t // tm, d // tn),
        in_specs=o_specs + w_specs + [pl.BlockSpec((tm, tn), lambda i, j: (i, j))],
        out_specs=pl.BlockSpec((tm, tn), lambda i, j: (i, j)),
        out_shape=jax.ShapeDtypeStruct((t, d), F32),
        compiler_params=pltpu.CompilerParams(
            dimension_semantics=("parallel", "arbitrary"),
            vmem_limit_bytes=VMEM_LIMIT_BYTES),
        name="out_proj",
    )(*parts, *([w] * len(parts)), x)


def _attn_kernel(*refs, tiles, use_sink, plans):
    refs = list(refs)
    q_ref, k_ref, v_ref, table_ref = refs[:4]
    sink_ref = refs[4] if use_sink else None
    o_ref = refs[4 + use_sink]
    if use_sink:
        sink = sink_ref[0, :, 0:1]

    if plans is None:
        bias_tile = lambda case, c0, kw: table_ref[0, case, :, c0:c0 + kw]
    else:
        bias_ref = refs[5 + use_sink]
        w = GRID_W

        @pl.when(pl.program_id(1) == 0)
        def _():
            for case, plan in enumerate(plans):
                for qr, row in enumerate(plan):
                    for kp, block in enumerate(row):
                        bias_ref[case, qr * w:(qr + 1) * w, kp * 2 * w:(kp + 1) * 2 * w] = \
                            table_ref[0, block]

        bias_tile = lambda case, c0, kw: bias_ref[case, :, c0:c0 + kw]

    for q0, bq, k0, kw, case, c0 in tiles:
        q = q_ref[q0:q0 + bq, :]
        k = k_ref[k0:k0 + kw, :]
        s = lax.dot_general(q, k, (((1,), (1,)), ((), ())), preferred_element_type=F32)
        s = s + bias_tile(case, c0, kw)
        m = jnp.max(s, axis=-1, keepdims=True)
        if use_sink:
            m = jnp.maximum(m, sink)
        p = jnp.exp(s - m)
        denom = jnp.sum(p, axis=-1, keepdims=True)
        if use_sink:
            denom = denom + jnp.exp(sink - m)
        o = jnp.dot(p.astype(BF16), v_ref[k0:k0 + kw, :], preferred_element_type=F32)
        o_ref[q0:q0 + bq, :] = (o / denom).astype(o_ref.dtype)


def _attention(qkv, table, sink, *, batch, seq, n_heads, group, q_col, k_col, v_col, tiles,
               plans=None):
    dh = HEAD_DIM
    use_sink = sink is not None
    in_specs = [
        pl.BlockSpec((seq, dh), lambda h, b: (b, q_col + h)),
        pl.BlockSpec((seq, dh), lambda h, b: (b, k_col + h // group)),
        pl.BlockSpec((seq, dh), lambda h, b: (b, v_col + h // group)),
        pl.BlockSpec((1,) + table.shape[1:], lambda h, b: (h, 0, 0, 0)),
    ]
    args = [qkv, qkv, qkv, table]
    if use_sink:
        in_specs.append(pl.BlockSpec((1, 1, dh), lambda h, b: (h, 0, 0)))
        args.append(jnp.broadcast_to(sink.astype(F32)[:, None, None], (n_heads, 1, dh)))
    scratch = []
    if plans is not None:
        q_rows, k_pairs = len(plans[0]), len(plans[0][0])
        scratch.append(pltpu.VMEM((len(plans), q_rows * GRID_W, k_pairs * 2 * GRID_W), F32))
    return pl.pallas_call(
        functools.partial(_attn_kernel, tiles=tuple(tiles), use_sink=use_sink, plans=plans),
        grid=(n_heads, batch),
        in_specs=in_specs,
        out_specs=pl.BlockSpec((seq, dh), lambda h, b: (b, h)),
        out_shape=jax.ShapeDtypeStruct((batch * seq, n_heads * dh), BF16),
        scratch_shapes=scratch,
        compiler_params=pltpu.CompilerParams(
            dimension_semantics=("parallel", "arbitrary"),
            vmem_limit_bytes=VMEM_LIMIT_BYTES),
        name="attention",
    )(*args)


def _alibi_slopes(n):
    return 2.0 ** (-8.0 * jnp.arange(1, n + 1, dtype=F32) / n)


def _toeplitz_tiles(seq, bq, halo):
    tiles = []
    for q0 in range(0, seq, bq):
        k0, k1 = max(0, q0 - halo), min(seq, q0 + bq + halo)
        tiles.append((q0, bq, k0, k1 - k0, 0, k0 - q0 + halo))
    return tiles


def _toeplitz_offsets(bq, halo):
    return (jnp.arange(bq + 2 * halo, dtype=jnp.int32)[None, :] - halo
            - jnp.arange(bq, dtype=jnp.int32)[:, None])


def _dilated_bias(bq, halo):
    d = _toeplitz_offsets(bq, halo)
    ad = jnp.abs(d)
    count = jnp.zeros(d.shape, F32)
    for window, r in A_PATTERNS:
        count = count + ((d % r == 0) & (ad <= (window // (2 * r)) * r)).astype(F32)
    log_count = jnp.where(count > 0, jnp.log(jnp.maximum(count, 1.0)), NEG_INF)
    slopes = _alibi_slopes(A_HEADS)
    return (log_count[None] - slopes[:, None, None] * ad.astype(F32)[None])[:, None]


def _banded_bias(bq, halo):
    d = _toeplitz_offsets(bq, halo)
    ad = jnp.abs(d)
    slopes = _alibi_slopes(B_HEADS)
    bias = jnp.where((ad <= B_HALF_WINDOW)[None], -slopes[:, None, None] * ad.astype(F32)[None], NEG_INF)
    return bias[:, None]


def _neighbourhood_geometry(seq, q_rows, k_rows):
    rows = seq // GRID_W
    kr = min(C_KR_MAX, rows)
    k_rows = min(k_rows, rows)
    assert rows % q_rows == 0 and k_rows % 2 == 0 and k_rows >= min(rows, q_rows + kr - 1)
    n_r, n_c = 2 * C_KR_MAX - 1, 2 * C_KC - 1
    qc, kc = np.arange(GRID_W)[:, None], np.arange(GRID_W)[None, :]
    cs = np.clip(qc - C_KC // 2, 0, GRID_W - C_KC)
    in_c = (kc >= cs) & (kc < cs + C_KC)
    ci = np.where(in_c, np.clip(kc - qc + C_KC - 1, 0, n_c - 1), n_c)
    col_sel = (ci[..., None] == np.arange(n_c + 1)).astype(np.float32)
    blocks, cases, tiles = {}, {}, []
    for r0 in range(0, rows, q_rows):
        kstart = int(np.clip(r0 - kr // 2, 0, rows - k_rows))
        qr = r0 + np.arange(q_rows)[:, None]
        krow = kstart + np.arange(k_rows)[None, :]
        rs = np.clip(qr - kr // 2, 0, rows - kr)
        in_r = (krow >= rs) & (krow < rs + kr)
        ri = np.where(in_r, np.clip(krow - qr + C_KR_MAX - 1, 0, n_r - 1), n_r)
        plan = tuple(tuple(blocks.setdefault((int(ri[q, 2 * p]), int(ri[q, 2 * p + 1])), len(blocks))
                           for p in range(k_rows // 2)) for q in range(q_rows))
        case = cases.setdefault(plan, len(cases))
        tiles.append((r0 * GRID_W, q_rows * GRID_W, kstart * GRID_W, k_rows * GRID_W, case, 0))
    row_pair_sel = np.zeros((len(blocks), 2, n_r + 1), np.float32)
    for (i0, i1), b in blocks.items():
        row_pair_sel[b, 0, i0] = 1.0
        row_pair_sel[b, 1, i1] = 1.0
    plans = tuple(sorted(cases, key=cases.get))
    return tiles, plans, (row_pair_sel, col_sel)


def _neighbourhood_blocks(rpb, selectors):
    row_pair_sel, col_sel = selectors
    hp = lax.Precision.HIGHEST
    table = jnp.pad(rpb.astype(F32), ((0, 0), (0, 1), (0, 1)), constant_values=NEG_INF)
    t = jnp.einsum('hij,psi->hpsj', table, row_pair_sel, precision=hp)
    t = jnp.einsum('hpsj,xyj->hpxsy', t, col_sel, precision=hp)
    return t.reshape(t.shape[0], t.shape[1], GRID_W, 2 * GRID_W)


def kernel(x, ffn_norm, ffn_w_gate, ffn_w_up, ffn_w_down, mix_norm, ab_w_in, ab_w_out, ab_sink,
           c_w_in, c_w_out, c_rpb, final_norm):
    batch, seq, d = x.shape
    depth = ffn_norm.shape[0]
    dh = HEAD_DIM
    scale = dh ** -0.5

    da, db, dkv = A_HEADS * dh, B_HEADS * dh, B_KV_HEADS * dh
    ab_scale = jnp.concatenate([jnp.full((da,), scale, F32), jnp.ones((2 * da,), F32),
                                jnp.full((db,), scale, F32), jnp.ones((2 * dkv,), F32)])
    dc = C_HEADS * dh
    c_scale = jnp.concatenate([jnp.full((dc,), scale, F32), jnp.ones((2 * dc,), F32)])

    bq = min(ATTN_Q_BLOCK, seq)
    halo_a = max((w // (2 * r)) * r for w, r in A_PATTERNS)
    tiles_a, bias_a = _toeplitz_tiles(seq, bq, halo_a), _dilated_bias(bq, halo_a)
    tiles_b, bias_b = _toeplitz_tiles(seq, bq, B_HALF_WINDOW), _banded_bias(bq, B_HALF_WINDOW)
    rows = seq // GRID_W
    q_rows = math.gcd(rows, NEIGHBOURHOOD_Q_ROWS)
    tiles_c, plans_c, selectors_c = _neighbourhood_geometry(seq, q_rows, q_rows + C_KR_MAX)

    ffn_gain = ffn_norm.reshape(depth, 2, 1, d)
    mix_gain = mix_norm.reshape(depth, 1, d)

    def ffn_sources(layer, half):
        return [Weight(w, (layer, half)) for w in (ffn_w_gate, ffn_w_up, ffn_w_down)]

    ffn_weights = [Weight(w.arr[w.prefix].astype(BF16), ()) for w in ffn_sources(0, 0)]

    xf = x.reshape(batch * seq, d)
    for layer in range(depth):
        i = layer // 2
        mixer_sources = ([Weight(ab_w_in, (i,)), Weight(ab_w_out, (i,))] if layer % 2 == 0
                         else [Weight(c_w_in, (i,)), Weight(c_w_out, (i,))])
        xf, cast = _ffn(xf, Weight(ffn_gain, (layer, 0)), *ffn_weights, final_norm,
                        ffn_sources(layer, 1) + mixer_sources, final_norm=False)
        ffn_weights = [Weight(w, ()) for w in cast[:3]]
        w_in, w_out = cast[3:]
        gain = Weight(mix_gain, (layer,))
        if layer % 2 == 0:
            proj = _in_proj(xf, gain, w_in, ab_scale)
            oa = _attention(proj, bias_a, None, batch=batch, seq=seq, n_heads=A_HEADS, group=1,
                            q_col=0, k_col=A_HEADS, v_col=2 * A_HEADS, tiles=tiles_a)
            ob = _attention(proj, bias_b, ab_sink[i], batch=batch, seq=seq, n_heads=B_HEADS,
                            group=B_HEADS // B_KV_HEADS, q_col=3 * A_HEADS,
                            k_col=3 * A_HEADS + B_HEADS, v_col=3 * A_HEADS + B_HEADS + B_KV_HEADS,
                            tiles=tiles_b)
            xf = _out_proj([oa, ob], w_out, xf)
        else:
            proj = _in_proj(xf, gain, w_in, c_scale)
            oc = _attention(proj, _neighbourhood_blocks(c_rpb[i], selectors_c), None, batch=batch,
                            seq=seq, n_heads=C_HEADS, group=1, q_col=0, k_col=C_HEADS,
                            v_col=2 * C_HEADS, tiles=tiles_c, plans=plans_c)
            xf = _out_proj([oc], w_out, xf)
        last = layer == depth - 1
        xf, cast = _ffn(xf, Weight(ffn_gain, (layer, 1)), *ffn_weights, final_norm,
                        [] if last else ffn_sources(layer + 1, 0), final_norm=last)
        ffn_weights = [Weight(w, ()) for w in cast]
    return xf.reshape(batch, seq, d)
```

```python
import collections
import functools
import math

import numpy as np
import jax
import jax.numpy as jnp
from jax import lax
from jax.experimental import pallas as pl
from jax.experimental.pallas import tpu as pltpu

HEAD_DIM = 128
MACARON_WEIGHT = 0.5
RMS_EPS = 1e-6
NEG_INF = -1e30
LOG2_E = math.log2(math.e)

A_HEADS = 8
A_PATTERNS = ((128, 1), (512, 4), (2048, 16))
B_HEADS = 8
B_KV_HEADS = 2
B_HALF_WINDOW = 128
C_HEADS = 16
GRID_W = 64
C_KR_MAX = 8
C_KC = 16

LANES = 128
VMEM_LIMIT_BYTES = 56 * 1024 * 1024

TOKEN_BLOCK = 1024
FFN_HIDDEN_BLOCK = 512
IN_PROJ_COL_BLOCK = 1536
OUT_PROJ_COL_BLOCK = 1024
DILATED_Q_BLOCK = 512
BANDED_Q_BLOCK = 256
NEIGHBOURHOOD_Q_ROWS = 2

BF16 = jnp.bfloat16
F32 = jnp.float32

Weight = collections.namedtuple("Weight", "arr prefix")


def _weight_shape(w):
    return w.arr.shape[len(w.prefix):]


def _weight_spec(w, block, index_fn):
    prefix = tuple(w.prefix)
    return pl.BlockSpec((None,) * len(prefix) + tuple(block), lambda i, j: prefix + tuple(index_fn(i, j)))


def _col_block(n, cap):
    best = max(c for c in range(LANES, min(n, cap) + 1, LANES) if n % c == 0)
    return best


def _rms_normalize(x, g):
    ms = jnp.mean(x * x, axis=-1, keepdims=True)
    return x * lax.rsqrt(ms + RMS_EPS) * g


def _ffn_kernel(*refs, n_token_blocks, n_hidden_blocks, down_col_block, final_norm, n_casts):
    x_hbm, g_ref, wg_ref, wu_ref, wd_ref, gf_ref = refs[:6]
    src_refs = refs[6:6 + n_casts]
    o_ref = refs[6 + n_casts]
    dst_refs = refs[7 + n_casts:7 + 2 * n_casts]
    h_ref, x_buf, x_sem = refs[7 + 2 * n_casts:]
    i, j = pl.program_id(0), pl.program_id(1)
    tm = x_buf.shape[0]

    def x_copy(block):
        return pltpu.make_async_copy(x_hbm.at[pl.ds(block * tm, tm), :], x_buf, x_sem)

    @pl.when((i == 0) & (j == 0))
    def _():
        x_copy(0).start()

    @pl.when(j == 0)
    def _():
        x_copy(i).wait()
        x = x_buf[...]
        h_ref[...] = _rms_normalize(x, g_ref[...]).astype(BF16)
        o_ref[...] = x

    @pl.when((j == 1) & (i + 1 < n_token_blocks))
    def _():
        x_copy(i + 1).start()

    for src_ref, dst_ref in zip(src_refs, dst_refs):
        dst_ref[...] = src_ref[...].astype(BF16)

    h = h_ref[...]
    gate = jnp.dot(h, wg_ref[...], preferred_element_type=F32)
    up = jnp.dot(h, wu_ref[...], preferred_element_type=F32)
    act = (MACARON_WEIGHT * (gate * jax.nn.sigmoid(gate)) * up).astype(BF16)
    d_model = o_ref.shape[1]
    for c0 in range(0, d_model, down_col_block):
        cols = slice(c0, c0 + down_col_block)
        o_ref[:, cols] += jnp.dot(act, wd_ref[:, cols], preferred_element_type=F32)

    if final_norm:
        @pl.when(j == n_hidden_blocks - 1)
        def _():
            o_ref[...] = _rms_normalize(o_ref[...], gf_ref[...])


def _ffn(x, g, wg, wu, wd, g_final, casts, *, final_norm):
    t, d = x.shape
    f = _weight_shape(wg)[1]
    tm, tf = min(TOKEN_BLOCK, t), min(FFN_HIDDEN_BLOCK, f)
    assert t % tm == 0 and f % tf == 0
    ni, nj = t // tm, f // tf
    assert nj >= 2

    cast_in, cast_out, cast_shapes = [], [], []
    for w in casts:
        r, c = _weight_shape(w)
        assert r % (16 * ni) == 0
        rb = r // ni
        cb = min(cb for cb in range(LANES, c + 1, LANES) if c % cb == 0 and c // cb <= nj)
        last = c // cb - 1
        index_fn = functools.partial(lambda i, j, last: (i, jnp.minimum(j, last)), last=last)
        cast_in.append(_weight_spec(w, (rb, cb), index_fn))
        cast_out.append(pl.BlockSpec((rb, cb), index_fn))
        cast_shapes.append(jax.ShapeDtypeStruct((r, c), BF16))

    kern = functools.partial(_ffn_kernel, n_token_blocks=ni, n_hidden_blocks=nj,
                             down_col_block=min(512, d), final_norm=final_norm, n_casts=len(casts))
    outs = pl.pallas_call(
        kern,
        grid=(ni, nj),
        in_specs=[
            pl.BlockSpec(memory_space=pl.ANY),
            _weight_spec(g, (1, d), lambda i, j: (0, 0)),
            _weight_spec(wg, (d, tf), lambda i, j: (0, j)),
            _weight_spec(wu, (d, tf), lambda i, j: (0, j)),
            _weight_spec(wd, (tf, d), lambda i, j: (j, 0)),
            pl.BlockSpec((1, d), lambda i, j: (0, 0)),
        ] + cast_in,
        out_specs=[pl.BlockSpec((tm, d), lambda i, j: (i, 0))] + cast_out,
        out_shape=[jax.ShapeDtypeStruct((t, d), F32)] + cast_shapes,
        scratch_shapes=[pltpu.VMEM((tm, d), BF16), pltpu.VMEM((tm, d), F32),
                        pltpu.SemaphoreType.DMA(())],
        compiler_params=pltpu.CompilerParams(
            dimension_semantics=("arbitrary", "arbitrary"),
            vmem_limit_bytes=VMEM_LIMIT_BYTES),
        name="ffn",
    )(x, g.arr, wg.arr, wu.arr, wd.arr, g_final.reshape(1, d), *[w.arr for w in casts])
    return outs[0], list(outs[1:])


def _in_proj_kernel(x_ref, g_ref, w_ref, s_ref, o_ref, h_ref):
    @pl.when(pl.program_id(1) == 0)
    def _():
        h_ref[...] = _rms_normalize(x_ref[...], g_ref[...]).astype(BF16)

    acc = jnp.dot(h_ref[...], w_ref[...], preferred_element_type=F32)
    o_ref[...] = (acc * s_ref[...]).astype(o_ref.dtype)


def _in_proj(x, g, w, col_scale):
    t, d = x.shape
    n = w.shape[1]
    tm, tn = min(TOKEN_BLOCK, t), _col_block(n, IN_PROJ_COL_BLOCK)
    assert t % tm == 0
    return pl.pallas_call(
        _in_proj_kernel,
        grid=(t // tm, n // tn),
        in_specs=[
            pl.BlockSpec((tm, d), lambda i, j: (i, 0)),
            _weight_spec(g, (1, d), lambda i, j: (0, 0)),
            pl.BlockSpec((d, tn), lambda i, j: (0, j)),
            pl.BlockSpec((1, tn), lambda i, j: (0, j)),
        ],
        out_specs=pl.BlockSpec((tm, tn), lambda i, j: (i, j)),
        out_shape=jax.ShapeDtypeStruct((t, n), BF16),
        scratch_shapes=[pltpu.VMEM((tm, d), BF16)],
        compiler_params=pltpu.CompilerParams(
            dimension_semantics=("parallel", "arbitrary"),
            vmem_limit_bytes=VMEM_LIMIT_BYTES),
        name="in_proj",
    )(x, g.arr, w, col_scale.reshape(1, n))


def _out_proj_kernel(*refs, n_parts):
    o_refs, w_refs = refs[:n_parts], refs[n_parts:2 * n_parts]
    x_ref, out_ref = refs[2 * n_parts], refs[2 * n_parts + 1]
    acc = x_ref[...]
    for o_ref, w_ref in zip(o_refs, w_refs):
        acc = acc + jnp.dot(o_ref[...], w_ref[...], preferred_element_type=F32)
    out_ref[...] = acc


def _out_proj(parts, w, x):
    t, d = x.shape
    tm, tn = min(TOKEN_BLOCK, t), _col_block(d, OUT_PROJ_COL_BLOCK)
    widths = [p.shape[1] for p in parts]
    assert sum(widths) == w.shape[0] and len(set(widths)) == 1
    width = widths[0]
    o_specs = [pl.BlockSpec((tm, width), lambda i, j: (i, 0)) for _ in parts]
    w_specs = [pl.BlockSpec((width, tn), functools.partial(lambda i, j, p: (p, j), p=p))
               for p in range(len(parts))]
    return pl.pallas_call(
        functools.partial(_out_proj_kernel, n_parts=len(parts)),
        grid=(t // tm, d // tn),
        in_specs=o_specs + w_specs + [pl.BlockSpec((tm, tn), lambda i, j: (i, j))],
        out_specs=pl.BlockSpec((tm, tn), lambda i, j: (i, j)),
        out_shape=jax.ShapeDtypeStruct((t, d), F32),
        compiler_params=pltpu.CompilerParams(
            dimension_semantics=("parallel", "arbitrary"),
            vmem_limit_bytes=VMEM_LIMIT_BYTES),
        name="out_proj",
    )(*parts, *([w] * len(parts)), x)


def _attn_kernel(*refs, tiles, use_sink, plans):
    refs = list(refs)
    q_ref, k_ref, v_ref, table_ref = refs[:4]
    sink_ref = refs[4] if use_sink else None
    o_ref = refs[4 + use_sink]
    if use_sink:
        sink = sink_ref[0, :, 0:1]

    if plans is None:
        bias_tile = lambda case, c0, kw: table_ref[0, case, :, c0:c0 + kw]
    else:
        bias_ref = refs[5 + use_sink]
        w = GRID_W

        @pl.when(pl.program_id(1) == 0)
        def _():
            for case, plan in enumerate(plans):
                for qr, row in enumerate(plan):
                    for kp, block in enumerate(row):
                        bias_ref[case, qr * w:(qr + 1) * w, kp * 2 * w:(kp + 1) * 2 * w] = \
                            table_ref[0, block]

        bias_tile = lambda case, c0, kw: bias_ref[case, :, c0:c0 + kw]

    dh = v_ref.shape[1]
    vx_ref = refs[-1]
    vx_ref[:, :dh] = v_ref[...]
    vx_ref[:, dh:] = jnp.ones((v_ref.shape[0], dh), BF16)
    for q0, bq, k0, kw, case, c0 in tiles:
        q = q_ref[q0:q0 + bq, :]
        k = k_ref[k0:k0 + kw, :]
        s = lax.dot_general(q, k, (((1,), (1,)), ((), ())), preferred_element_type=F32)
        s = s + bias_tile(case, c0, kw)
        m = jnp.max(s, axis=-1, keepdims=True)
        if use_sink:
            m = jnp.maximum(m, sink)
        p = jnp.exp2(s - m)
        o = jnp.dot(p.astype(BF16), vx_ref[k0:k0 + kw, :], preferred_element_type=F32)
        denom = o[:, dh:]
        if use_sink:
            denom = denom + jnp.exp2(sink - m)
        o_ref[q0:q0 + bq, :] = (o[:, :dh] / denom).astype(o_ref.dtype)


def _attention(qkv, table, sink, *, batch, seq, n_heads, group, q_col, k_col, v_col, tiles,
               plans=None):
    dh = HEAD_DIM
    use_sink = sink is not None
    in_specs = [
        pl.BlockSpec((seq, dh), lambda h, b: (b, q_col + h)),
        pl.BlockSpec((seq, dh), lambda h, b: (b, k_col + h // group)),
        pl.BlockSpec((seq, dh), lambda h, b: (b, v_col + h // group)),
        pl.BlockSpec((1,) + table.shape[1:], lambda h, b: (h, 0, 0, 0)),
    ]
    args = [qkv, qkv, qkv, table]
    if use_sink:
        in_specs.append(pl.BlockSpec((1, 1, dh), lambda h, b: (h, 0, 0)))
        args.append(jnp.broadcast_to(LOG2_E * sink.astype(F32)[:, None, None], (n_heads, 1, dh)))
    scratch = []
    if plans is not None:
        q_rows, k_pairs = len(plans[0]), len(plans[0][0])
        scratch.append(pltpu.VMEM((len(plans), q_rows * GRID_W, k_pairs * 2 * GRID_W), F32))
    scratch.append(pltpu.VMEM((seq, 2 * dh), BF16))
    return pl.pallas_call(
        functools.partial(_attn_kernel, tiles=tuple(tiles), use_sink=use_sink, plans=plans),
        grid=(n_heads, batch),
        in_specs=in_specs,
        out_specs=pl.BlockSpec((seq, dh), lambda h, b: (b, h)),
        out_shape=jax.ShapeDtypeStruct((batch * seq, n_heads * dh), BF16),
        scratch_shapes=scratch,
        compiler_params=pltpu.CompilerParams(
            dimension_semantics=("parallel", "arbitrary"),
            vmem_limit_bytes=VMEM_LIMIT_BYTES),
        name="attention",
    )(*args)


def _alibi_slopes(n):
    return 2.0 ** (-8.0 * jnp.arange(1, n + 1, dtype=F32) / n)


def _toeplitz_tiles(seq, bq, halo):
    tiles = []
    for q0 in range(0, seq, bq):
        k0, k1 = max(0, q0 - halo), min(seq, q0 + bq + halo)
        tiles.append((q0, bq, k0, k1 - k0, 0, k0 - q0 + halo))
    return tiles


def _toeplitz_offsets(bq, halo):
    return (jnp.arange(bq + 2 * halo, dtype=jnp.int32)[None, :] - halo
            - jnp.arange(bq, dtype=jnp.int32)[:, None])


def _dilated_bias(bq, halo):
    d = _toeplitz_offsets(bq, halo)
    ad = jnp.abs(d)
    count = jnp.zeros(d.shape, F32)
    for window, r in A_PATTERNS:
        count = count + ((d % r == 0) & (ad <= (window // (2 * r)) * r)).astype(F32)
    log_count = jnp.where(count > 0, jnp.log(jnp.maximum(count, 1.0)), NEG_INF)
    slopes = _alibi_slopes(A_HEADS)
    return (LOG2_E * (log_count[None] - slopes[:, None, None] * ad.astype(F32)[None]))[:, None]


def _banded_bias(bq, halo):
    d = _toeplitz_offsets(bq, halo)
    ad = jnp.abs(d)
    slopes = _alibi_slopes(B_HEADS)
    bias = jnp.where((ad <= B_HALF_WINDOW)[None], -slopes[:, None, None] * ad.astype(F32)[None], NEG_INF)
    return LOG2_E * bias[:, None]


def _neighbourhood_geometry(seq, q_rows, k_rows):
    rows = seq // GRID_W
    kr = min(C_KR_MAX, rows)
    k_rows = min(k_rows, rows)
    assert rows % q_rows == 0 and k_rows % 2 == 0 and k_rows >= min(rows, q_rows + kr - 1)
    n_r, n_c = 2 * C_KR_MAX - 1, 2 * C_KC - 1
    qc, kc = np.arange(GRID_W)[:, None], np.arange(GRID_W)[None, :]
    cs = np.clip(qc - C_KC // 2, 0, GRID_W - C_KC)
    in_c = (kc >= cs) & (kc < cs + C_KC)
    ci = np.where(in_c, np.clip(kc - qc + C_KC - 1, 0, n_c - 1), n_c)
    col_sel = (ci[..., None] == np.arange(n_c + 1)).astype(np.float32)
    blocks, cases, tiles = {}, {}, []
    for r0 in range(0, rows, q_rows):
        kstart = int(np.clip(r0 - kr // 2, 0, rows - k_rows))
        qr = r0 + np.arange(q_rows)[:, None]
        krow = kstart + np.arange(k_rows)[None, :]
        rs = np.clip(qr - kr // 2, 0, rows - kr)
        in_r = (krow >= rs) & (krow < rs + kr)
        ri = np.where(in_r, np.clip(krow - qr + C_KR_MAX - 1, 0, n_r - 1), n_r)
        plan = tuple(tuple(blocks.setdefault((int(ri[q, 2 * p]), int(ri[q, 2 * p + 1])), len(blocks))
                           for p in range(k_rows // 2)) for q in range(q_rows))
        case = cases.setdefault(plan, len(cases))
        tiles.append((r0 * GRID_W, q_rows * GRID_W, kstart * GRID_W, k_rows * GRID_W, case, 0))
    row_pair_sel = np.zeros((len(blocks), 2, n_r + 1), np.float32)
    for (i0, i1), b in blocks.items():
        row_pair_sel[b, 0, i0] = 1.0
        row_pair_sel[b, 1, i1] = 1.0
    plans = tuple(sorted(cases, key=cases.get))
    return tiles, plans, (row_pair_sel, col_sel)


def _neighbourhood_blocks(rpb, selectors):
    row_pair_sel, col_sel = selectors
    hp = lax.Precision.HIGHEST
    table = jnp.pad(LOG2_E * rpb.astype(F32), ((0, 0), (0, 1), (0, 1)), constant_values=NEG_INF)
    t = jnp.einsum('hij,psi->hpsj', table, row_pair_sel, precision=hp)
    t = jnp.einsum('hpsj,xyj->hpxsy', t, col_sel, precision=hp)
    return t.reshape(t.shape[0], t.shape[1], GRID_W, 2 * GRID_W)


def kernel(x, ffn_norm, ffn_w_gate, ffn_w_up, ffn_w_down, mix_norm, ab_w_in, ab_w_out, ab_sink,
           c_w_in, c_w_out, c_rpb, final_norm):
    batch, seq, d = x.shape
    depth = ffn_norm.shape[0]
    dh = HEAD_DIM
    scale = LOG2_E * dh ** -0.5

    da, db, dkv = A_HEADS * dh, B_HEADS * dh, B_KV_HEADS * dh
    ab_scale = jnp.concatenate([jnp.full((da,), scale, F32), jnp.ones((2 * da,), F32),
                                jnp.full((db,), scale, F32), jnp.ones((2 * dkv,), F32)])
    dc = C_HEADS * dh
    c_scale = jnp.concatenate([jnp.full((dc,), scale, F32), jnp.ones((2 * dc,), F32)])

    bq_a, bq_b = min(DILATED_Q_BLOCK, seq), min(BANDED_Q_BLOCK, seq)
    halo_a = max((w // (2 * r)) * r for w, r in A_PATTERNS)
    tiles_a, bias_a = _toeplitz_tiles(seq, bq_a, halo_a), _dilated_bias(bq_a, halo_a)
    tiles_b, bias_b = _toeplitz_tiles(seq, bq_b, B_HALF_WINDOW), _banded_bias(bq_b, B_HALF_WINDOW)
    rows = seq // GRID_W
    q_rows = math.gcd(rows, NEIGHBOURHOOD_Q_ROWS)
    tiles_c, plans_c, selectors_c = _neighbourhood_geometry(seq, q_rows, q_rows + C_KR_MAX)

    ffn_gain = ffn_norm.reshape(depth, 2, 1, d)
    mix_gain = mix_norm.reshape(depth, 1, d)

    def ffn_sources(layer, half):
        return [Weight(w, (layer, half)) for w in (ffn_w_gate, ffn_w_up, ffn_w_down)]

    ffn_weights = [Weight(w.arr[w.prefix].astype(BF16), ()) for w in ffn_sources(0, 0)]

    xf = x.reshape(batch * seq, d)
    for layer in range(depth):
        i = layer // 2
        mixer_sources = ([Weight(ab_w_in, (i,)), Weight(ab_w_out, (i,))] if layer % 2 == 0
                         else [Weight(c_w_in, (i,)), Weight(c_w_out, (i,))])
        xf, cast = _ffn(xf, Weight(ffn_gain, (layer, 0)), *ffn_weights, final_norm,
                        ffn_sources(layer, 1) + mixer_sources, final_norm=False)
        ffn_weights = [Weight(w, ()) for w in cast[:3]]
        w_in, w_out = cast[3:]
        gain = Weight(mix_gain, (layer,))
        if layer % 2 == 0:
            proj = _in_proj(xf, gain, w_in, ab_scale)
            oa = _attention(proj, bias_a, None, batch=batch, seq=seq, n_heads=A_HEADS, group=1,
                            q_col=0, k_col=A_HEADS, v_col=2 * A_HEADS, tiles=tiles_a)
            ob = _attention(proj, bias_b, ab_sink[i], batch=batch, seq=seq, n_heads=B_HEADS,
                            group=B_HEADS // B_KV_HEADS, q_col=3 * A_HEADS,
                            k_col=3 * A_HEADS + B_HEADS, v_col=3 * A_HEADS + B_HEADS + B_KV_HEADS,
                            tiles=tiles_b)
            xf = _out_proj([oa, ob], w_out, xf)
        else:
            proj = _in_proj(xf, gain, w_in, c_scale)
            oc = _attention(proj, _neighbourhood_blocks(c_rpb[i], selectors_c), None, batch=batch,
                            seq=seq, n_heads=C_HEADS, group=1, q_col=0, k_col=C_HEADS,
                            v_col=2 * C_HEADS, tiles=tiles_c, plans=plans_c)
            xf = _out_proj([oc], w_out, xf)
        last = layer == depth - 1
        xf, cast = _ffn(xf, Weight(ffn_gain, (layer, 1)), *ffn_weights, final_norm,
                        [] if last else ffn_sources(layer + 1, 0), final_norm=last)
        ffn_weights = [Weight(w, ()) for w in cast]
    return xf.reshape(batch, seq, d)
```

```python
import collections
import functools
import math

import numpy as np
import jax
import jax.numpy as jnp
from jax import lax
from jax.experimental import pallas as pl
from jax.experimental.pallas import tpu as pltpu

HEAD_DIM = 128
MACARON_WEIGHT = 0.5
RMS_EPS = 1e-6
NEG_INF = -1e30
LOG2_E = math.log2(math.e)

A_HEADS = 8
A_PATTERNS = ((128, 1), (512, 4), (2048, 16))
B_HEADS = 8
B_KV_HEADS = 2
B_HALF_WINDOW = 128
C_HEADS = 16
GRID_W = 64
C_KR_MAX = 8
C_KC = 16

LANES = 128
VMEM_LIMIT_BYTES = 56 * 1024 * 1024

TOKEN_BLOCK = 1024
FFN_HIDDEN_BLOCK = 512
IN_PROJ_COL_BLOCK = 1536
OUT_PROJ_COL_BLOCK = 1024
DILATED_Q_BLOCK = 512
BANDED_Q_BLOCK = 256
NEIGHBOURHOOD_Q_ROWS = 2

BF16 = jnp.bfloat16
F32 = jnp.float32

Weight = collections.namedtuple("Weight", "arr prefix")


def _weight_shape(w):
    return w.arr.shape[len(w.prefix):]


def _weight_spec(w, block, index_fn):
    prefix = tuple(w.prefix)
    return pl.BlockSpec((None,) * len(prefix) + tuple(block), lambda i, j: prefix + tuple(index_fn(i, j)))


def _col_block(n, cap):
    best = max(c for c in range(LANES, min(n, cap) + 1, LANES) if n % c == 0)
    return best


def _rms_normalize(x, g):
    ms = jnp.mean(x * x, axis=-1, keepdims=True)
    return x * lax.rsqrt(ms + RMS_EPS) * g


def _ffn_kernel(*refs, n_token_blocks, n_hidden_blocks, down_col_block, final_norm, n_casts):
    x_hbm, g_ref, wg_ref, wu_ref, wd_ref, gf_ref = refs[:6]
    src_refs = refs[6:6 + n_casts]
    o_ref = refs[6 + n_casts]
    dst_refs = refs[7 + n_casts:7 + 2 * n_casts]
    h_ref, x_buf, x_sem = refs[7 + 2 * n_casts:]
    i, j = pl.program_id(0), pl.program_id(1)
    tm = x_buf.shape[0]

    def x_copy(block):
        return pltpu.make_async_copy(x_hbm.at[pl.ds(block * tm, tm), :], x_buf, x_sem)

    @pl.when((i == 0) & (j == 0))
    def _():
        x_copy(0).start()

    @pl.when(j == 0)
    def _():
        x_copy(i).wait()
        x = x_buf[...]
        h_ref[...] = _rms_normalize(x, g_ref[...]).astype(BF16)
        o_ref[...] = x

    @pl.when((j == 1) & (i + 1 < n_token_blocks))
    def _():
        x_copy(i + 1).start()

    for src_ref, dst_ref in zip(src_refs, dst_refs):
        dst_ref[...] = src_ref[...].astype(BF16)

    h = h_ref[...]
    gate = jnp.dot(h, wg_ref[...], preferred_element_type=F32)
    up = jnp.dot(h, wu_ref[...], preferred_element_type=F32)
    act = (MACARON_WEIGHT * (gate * jax.nn.sigmoid(gate)) * up).astype(BF16)
    d_model = o_ref.shape[1]
    for c0 in range(0, d_model, down_col_block):
        cols = slice(c0, c0 + down_col_block)
        o_ref[:, cols] += jnp.dot(act, wd_ref[:, cols], preferred_element_type=F32)

    if final_norm:
        @pl.when(j == n_hidden_blocks - 1)
        def _():
            o_ref[...] = _rms_normalize(o_ref[...], gf_ref[...])


def _ffn(x, g, wg, wu, wd, g_final, casts, *, final_norm):
    t, d = x.shape
    f = _weight_shape(wg)[1]
    tm, tf = min(TOKEN_BLOCK, t), min(FFN_HIDDEN_BLOCK, f)
    assert t % tm == 0 and f % tf == 0
    ni, nj = t // tm, f // tf
    assert nj >= 2

    cast_in, cast_out, cast_shapes = [], [], []
    for w in casts:
        r, c = _weight_shape(w)
        assert r % (16 * ni) == 0
        rb = r // ni
        cb = min(cb for cb in range(LANES, c + 1, LANES) if c % cb == 0 and c // cb <= nj)
        last = c // cb - 1
        index_fn = functools.partial(lambda i, j, last: (i, jnp.minimum(j, last)), last=last)
        cast_in.append(_weight_spec(w, (rb, cb), index_fn))
        cast_out.append(pl.BlockSpec((rb, cb), index_fn))
        cast_shapes.append(jax.ShapeDtypeStruct((r, c), BF16))

    kern = functools.partial(_ffn_kernel, n_token_blocks=ni, n_hidden_blocks=nj,
                             down_col_block=min(512, d), final_norm=final_norm, n_casts=len(casts))
    outs = pl.pallas_call(
        kern,
        grid=(ni, nj),
        in_specs=[
            pl.BlockSpec(memory_space=pl.ANY),
            _weight_spec(g, (1, d), lambda i, j: (0, 0)),
            _weight_spec(wg, (d, tf), lambda i, j: (0, j)),
            _weight_spec(wu, (d, tf), lambda i, j: (0, j)),
            _weight_spec(wd, (tf, d), lambda i, j: (j, 0)),
            pl.BlockSpec((1, d), lambda i, j: (0, 0)),
        ] + cast_in,
        out_specs=[pl.BlockSpec((tm, d), lambda i, j: (i, 0))] + cast_out,
        out_shape=[jax.ShapeDtypeStruct((t, d), F32)] + cast_shapes,
        scratch_shapes=[pltpu.VMEM((tm, d), BF16), pltpu.VMEM((tm, d), F32),
                        pltpu.SemaphoreType.DMA(())],
        compiler_params=pltpu.CompilerParams(
            dimension_semantics=("arbitrary", "arbitrary"),
            vmem_limit_bytes=VMEM_LIMIT_BYTES),
        name="ffn",
    )(x, g.arr, wg.arr, wu.arr, wd.arr, g_final.reshape(1, d), *[w.arr for w in casts])
    return outs[0], list(outs[1:])


def _in_proj_kernel(x_ref, g_ref, w_ref, s_ref, o_ref, h_ref):
    @pl.when(pl.program_id(1) == 0)
    def _():
        h_ref[...] = _rms_normalize(x_ref[...], g_ref[...]).astype(BF16)

    acc = jnp.dot(h_ref[...], w_ref[...], preferred_element_type=F32)
    o_ref[...] = (acc * s_ref[...]).astype(o_ref.dtype)


def _in_proj(x, g, w, col_scale):
    t, d = x.shape
    n = w.shape[1]
    tm, tn = min(TOKEN_BLOCK, t), _col_block(n, IN_PROJ_COL_BLOCK)
    assert t % tm == 0
    return pl.pallas_call(
        _in_proj_kernel,
        grid=(t // tm, n // tn),
        in_specs=[
            pl.BlockSpec((tm, d), lambda i, j: (i, 0)),
            _weight_spec(g, (1, d), lambda i, j: (0, 0)),
            pl.BlockSpec((d, tn), lambda i, j: (0, j)),
            pl.BlockSpec((1, tn), lambda i, j: (0, j)),
        ],
        out_specs=pl.BlockSpec((tm, tn), lambda i, j: (i, j)),
        out_shape=jax.ShapeDtypeStruct((t, n), BF16),
        scratch_shapes=[pltpu.VMEM((tm, d), BF16)],
        compiler_params=pltpu.CompilerParams(
            dimension_semantics=("parallel", "arbitrary"),
            vmem_limit_bytes=VMEM_LIMIT_BYTES),
        name="in_proj",
    )(x, g.arr, w, col_scale.reshape(1, n))


def _out_proj_kernel(*refs, n_parts):
    o_refs, w_refs = refs[:n_parts], refs[n_parts:2 * n_parts]
    x_ref, out_ref = refs[2 * n_parts], refs[2 * n_parts + 1]
    acc = x_ref[...]
    for o_ref, w_ref in zip(o_refs, w_refs):
        acc = acc + jnp.dot(o_ref[...], w_ref[...], preferred_element_type=F32)
    out_ref[...] = acc


def _out_proj(parts, w, x):
    t, d = x.shape
    tm, tn = min(TOKEN_BLOCK, t), _col_block(d, OUT_PROJ_COL_BLOCK)
    widths = [p.shape[1] for p in parts]
    assert sum(widths) == w.shape[0] and len(set(widths)) == 1
    width = widths[0]
    o_specs = [pl.BlockSpec((tm, width), lambda i, j: (i, 0)) for _ in parts]
    w_specs = [pl.BlockSpec((width, tn), functools.partial(lambda i, j, p: (p, j), p=p))
               for p in range(len(parts))]
    return pl.pallas_call(
        functools.partial(_out_proj_kernel, n_parts=len(parts)),
        grid=(t // tm, d // tn),
        in_specs=o_specs + w_specs + [pl.BlockSpec((tm, tn), lambda i, j: (i, j))],
        out_specs=pl.BlockSpec((tm, tn), lambda i, j: (i, j)),
        out_shape=jax.ShapeDtypeStruct((t, d), F32),
        compiler_params=pltpu.CompilerParams(
            dimension_semantics=("parallel", "arbitrary"),
            vmem_limit_bytes=VMEM_LIMIT_BYTES),
        name="out_proj",
    )(*parts, *([w] * len(parts)), x)


def _attn_kernel(*refs, tiles, use_sink, plans):
    refs = list(refs)
    q_ref, k_ref, v_ref, table_ref = refs[:4]
    sink_ref = refs[4] if use_sink else None
    o_ref = refs[4 + use_sink]
    if use_sink:
        sink = sink_ref[0, :, 0:1]

    if plans is None:
        bias_tile = lambda case, c0, kw: table_ref[0, case, :, c0:c0 + kw]
    else:
        bias_ref = refs[5 + use_sink]
        w = GRID_W

        @pl.when(pl.program_id(1) == 0)
        def _():
            for case, plan in enumerate(plans):
                for qr, row in enumerate(plan):
                    for kp, block in enumerate(row):
                        bias_ref[case, qr * w:(qr + 1) * w, kp * 2 * w:(kp + 1) * 2 * w] = \
                            table_ref[0, block]

        bias_tile = lambda case, c0, kw: bias_ref[case, :, c0:c0 + kw]

    dh = v_ref.shape[1]
    vx_ref = refs[-1]
    vx_ref[:, :dh] = v_ref[...]
    vx_ref[:, dh:] = jnp.ones((v_ref.shape[0], dh), BF16)
    for q0, bq, k0, kw, case, c0 in tiles:
        q = q_ref[q0:q0 + bq, :]
        k = k_ref[k0:k0 + kw, :]
        s = lax.dot_general(q, k, (((1,), (1,)), ((), ())), preferred_element_type=F32)
        s = s + bias_tile(case, c0, kw)
        m = jnp.max(s, axis=-1, keepdims=True)
        if use_sink:
            m = jnp.maximum(m, sink)
        p = jnp.exp2(s - m)
        o = jnp.dot(p.astype(BF16), vx_ref[k0:k0 + kw, :], preferred_element_type=F32)
        denom = o[:, dh:]
        if use_sink:
            denom = denom + jnp.exp2(sink - m)
        o_ref[q0:q0 + bq, :] = (o[:, :dh] / denom).astype(o_ref.dtype)


def _attention(qkv, table, sink, *, batch, seq, n_heads, group, q_col, k_col, v_col, tiles,
               plans=None):
    dh = HEAD_DIM
    use_sink = sink is not None
    in_specs = [
        pl.BlockSpec((seq, dh), lambda h, b: (b, q_col + h)),
        pl.BlockSpec((seq, dh), lambda h, b: (b, k_col + h // group)),
        pl.BlockSpec((seq, dh), lambda h, b: (b, v_col + h // group)),
        pl.BlockSpec((1,) + table.shape[1:], lambda h, b: (h, 0, 0, 0)),
    ]
    args = [qkv, qkv, qkv, table]
    if use_sink:
        in_specs.append(pl.BlockSpec((1, 1, dh), lambda h, b: (h, 0, 0)))
        args.append(jnp.broadcast_to(LOG2_E * sink.astype(F32)[:, None, None], (n_heads, 1, dh)))
    scratch = []
    if plans is not None:
        q_rows, k_pairs = len(plans[0]), len(plans[0][0])
        scratch.append(pltpu.VMEM((len(plans), q_rows * GRID_W, k_pairs * 2 * GRID_W), F32))
    scratch.append(pltpu.VMEM((seq, 2 * dh), BF16))
    return pl.pallas_call(
        functools.partial(_attn_kernel, tiles=tuple(tiles), use_sink=use_sink, plans=plans),
        grid=(n_heads, batch),
        in_specs=in_specs,
        out_specs=pl.BlockSpec((seq, dh), lambda h, b: (b, h)),
        out_shape=jax.ShapeDtypeStruct((batch * seq, n_heads * dh), BF16),
        scratch_shapes=scratch,
        compiler_params=pltpu.CompilerParams(
            dimension_semantics=("parallel", "arbitrary"),
            vmem_limit_bytes=VMEM_LIMIT_BYTES),
        name="attention",
    )(*args)


def _alibi_slopes(n):
    return 2.0 ** (-8.0 * jnp.arange(1, n + 1, dtype=F32) / n)


def _toeplitz_tiles(seq, bq, halo):
    tiles = []
    for q0 in range(0, seq, bq):
        k0, k1 = max(0, q0 - halo), min(seq, q0 + bq + halo)
        tiles.append((q0, bq, k0, k1 - k0, 0, k0 - q0 + halo))
    return tiles


def _toeplitz_offsets(bq, halo):
    return (jnp.arange(bq + 2 * halo, dtype=jnp.int32)[None, :] - halo
            - jnp.arange(bq, dtype=jnp.int32)[:, None])


def _dilated_bias(bq, halo):
    d = _toeplitz_offsets(bq, halo)
    ad = jnp.abs(d)
    count = jnp.zeros(d.shape, F32)
    for window, r in A_PATTERNS:
        count = count + ((d % r == 0) & (ad <= (window // (2 * r)) * r)).astype(F32)
    log_count = jnp.where(count > 0, jnp.log(jnp.maximum(count, 1.0)), NEG_INF)
    slopes = _alibi_slopes(A_HEADS)
    return (LOG2_E * (log_count[None] - slopes[:, None, None] * ad.astype(F32)[None]))[:, None]


def _banded_bias(bq, halo):
    d = _toeplitz_offsets(bq, halo)
    ad = jnp.abs(d)
    slopes = _alibi_slopes(B_HEADS)
    bias = jnp.where((ad <= B_HALF_WINDOW)[None], -slopes[:, None, None] * ad.astype(F32)[None], NEG_INF)
    return LOG2_E * bias[:, None]


def _neighbourhood_geometry(seq, q_rows, k_rows):
    rows = seq // GRID_W
    kr = min(C_KR_MAX, rows)
    k_rows = min(k_rows, rows)
    assert rows % q_rows == 0 and k_rows % 2 == 0 and k_rows >= min(rows, q_rows + kr - 1)
    n_r, n_c = 2 * C_KR_MAX - 1, 2 * C_KC - 1
    qc, kc = np.arange(GRID_W)[:, None], np.arange(GRID_W)[None, :]
    cs = np.clip(qc - C_KC // 2, 0, GRID_W - C_KC)
    in_c = (kc >= cs) & (kc < cs + C_KC)
    ci = np.where(in_c, np.clip(kc - qc + C_KC - 1, 0, n_c - 1), n_c)
    col_sel = (ci[..., None] == np.arange(n_c + 1)).astype(np.float32)
    blocks, cases, tiles = {}, {}, []
    for r0 in range(0, rows, q_rows):
        kstart = int(np.clip(r0 - kr // 2, 0, rows - k_rows))
        qr = r0 + np.arange(q_rows)[:, None]
        krow = kstart + np.arange(k_rows)[None, :]
        rs = np.clip(qr - kr // 2, 0, rows - kr)
        in_r = (krow >= rs) & (krow < rs + kr)
        ri = np.where(in_r, np.clip(krow - qr + C_KR_MAX - 1, 0, n_r - 1), n_r)
        plan = tuple(tuple(blocks.setdefault((int(ri[q, 2 * p]), int(ri[q, 2 * p + 1])), len(blocks))
                           for p in range(k_rows // 2)) for q in range(q_rows))
        case = cases.setdefault(plan, len(cases))
        tiles.append((r0 * GRID_W, q_rows * GRID_W, kstart * GRID_W, k_rows * GRID_W, case, 0))
    row_pair_sel = np.zeros((len(blocks), 2, n_r + 1), np.float32)
    for (i0, i1), b in blocks.items():
        row_pair_sel[b, 0, i0] = 1.0
        row_pair_sel[b, 1, i1] = 1.0
    plans = tuple(sorted(cases, key=cases.get))
    col_pair_sel = np.einsum('ts,xyj->tjxsy', np.eye(2, dtype=np.float32), col_sel)
    col_pair_sel = col_pair_sel.reshape(2 * (n_c + 1), GRID_W * 2 * GRID_W)
    return tiles, plans, (row_pair_sel, col_pair_sel)


def _neighbourhood_blocks(rpb, selectors):
    row_pair_sel, col_pair_sel = selectors
    hp = lax.Precision.HIGHEST
    table = jnp.pad(LOG2_E * rpb.astype(F32), ((0, 0), (0, 1), (0, 1)), constant_values=NEG_INF)
    t = jnp.einsum('hij,psi->hpsj', table, row_pair_sel, precision=hp)
    n_heads, n_blocks = t.shape[:2]
    t = jnp.dot(t.reshape(n_heads * n_blocks, -1), col_pair_sel, precision=hp)
    return t.reshape(n_heads, n_blocks, GRID_W, 2 * GRID_W)


def kernel(x, ffn_norm, ffn_w_gate, ffn_w_up, ffn_w_down, mix_norm, ab_w_in, ab_w_out, ab_sink,
           c_w_in, c_w_out, c_rpb, final_norm):
    batch, seq, d = x.shape
    depth = ffn_norm.shape[0]
    dh = HEAD_DIM
    scale = LOG2_E * dh ** -0.5

    da, db, dkv = A_HEADS * dh, B_HEADS * dh, B_KV_HEADS * dh
    ab_scale = jnp.concatenate([jnp.full((da,), scale, F32), jnp.ones((2 * da,), F32),
                                jnp.full((db,), scale, F32), jnp.ones((2 * dkv,), F32)])
    dc = C_HEADS * dh
    c_scale = jnp.concatenate([jnp.full((dc,), scale, F32), jnp.ones((2 * dc,), F32)])

    bq_a, bq_b = min(DILATED_Q_BLOCK, seq), min(BANDED_Q_BLOCK, seq)
    halo_a = max((w // (2 * r)) * r for w, r in A_PATTERNS)
    tiles_a, bias_a = _toeplitz_tiles(seq, bq_a, halo_a), _dilated_bias(bq_a, halo_a)
    tiles_b, bias_b = _toeplitz_tiles(seq, bq_b, B_HALF_WINDOW), _banded_bias(bq_b, B_HALF_WINDOW)
    rows = seq // GRID_W
    q_rows = math.gcd(rows, NEIGHBOURHOOD_Q_ROWS)
    tiles_c, plans_c, selectors_c = _neighbourhood_geometry(seq, q_rows, q_rows + C_KR_MAX)

    ffn_gain = ffn_norm.reshape(depth, 2, 1, d)
    mix_gain = mix_norm.reshape(depth, 1, d)

    def ffn_sources(layer, half):
        return [Weight(w, (layer, half)) for w in (ffn_w_gate, ffn_w_up, ffn_w_down)]

    ffn_weights = [Weight(w.arr[w.prefix].astype(BF16), ()) for w in ffn_sources(0, 0)]

    xf = x.reshape(batch * seq, d)
    for layer in range(depth):
        i = layer // 2
        mixer_sources = ([Weight(ab_w_in, (i,)), Weight(ab_w_out, (i,))] if layer % 2 == 0
                         else [Weight(c_w_in, (i,)), Weight(c_w_out, (i,))])
        xf, cast = _ffn(xf, Weight(ffn_gain, (layer, 0)), *ffn_weights, final_norm,
                        ffn_sources(layer, 1) + mixer_sources, final_norm=False)
        ffn_weights = [Weight(w, ()) for w in cast[:3]]
        w_in, w_out = cast[3:]
        gain = Weight(mix_gain, (layer,))
        if layer % 2 == 0:
            proj = _in_proj(xf, gain, w_in, ab_scale)
            oa = _attention(proj, bias_a, None, batch=batch, seq=seq, n_heads=A_HEADS, group=1,
                            q_col=0, k_col=A_HEADS, v_col=2 * A_HEADS, tiles=tiles_a)
            ob = _attention(proj, bias_b, ab_sink[i], batch=batch, seq=seq, n_heads=B_HEADS,
                            group=B_HEADS // B_KV_HEADS, q_col=3 * A_HEADS,
                            k_col=3 * A_HEADS + B_HEADS, v_col=3 * A_HEADS + B_HEADS + B_KV_HEADS,
                            tiles=tiles_b)
            xf = _out_proj([oa, ob], w_out, xf)
        else:
            proj = _in_proj(xf, gain, w_in, c_scale)
            oc = _attention(proj, _neighbourhood_blocks(c_rpb[i], selectors_c), None, batch=batch,
                            seq=seq, n_heads=C_HEADS, group=1, q_col=0, k_col=C_HEADS,
                            v_col=2 * C_HEADS, tiles=tiles_c, plans=plans_c)
            xf = _out_proj([oc], w_out, xf)
        last = layer == depth - 1
        xf, cast = _ffn(xf, Weight(ffn_gain, (layer, 1)), *ffn_weights, final_norm,
                        [] if last else ffn_sources(layer + 1, 0), final_norm=last)
        ffn_weights = [Weight(w, ()) for w in cast]
    return xf.reshape(batch, seq, d)
```

```python
import collections
import functools
import math

import numpy as np
import jax
import jax.numpy as jnp
from jax import lax
from jax.experimental import pallas as pl
from jax.experimental.pallas import tpu as pltpu

HEAD_DIM = 128
MACARON_WEIGHT = 0.5
RMS_EPS = 1e-6
NEG_INF = -1e30
LOG2_E = math.log2(math.e)

A_HEADS = 8
A_PATTERNS = ((128, 1), (512, 4), (2048, 16))
B_HEADS = 8
B_KV_HEADS = 2
B_HALF_WINDOW = 128
C_HEADS = 16
GRID_W = 64
C_KR_MAX = 8
C_KC = 16

LANES = 128
MXU_WIDTH = 256
VMEM_LIMIT_BYTES = 60 * 1024 * 1024

TOKEN_BLOCK = 1024
FFN_HIDDEN_BLOCK = 512
IN_PROJ_COL_BLOCK = 1024
OUT_PROJ_COL_BLOCK = 1024
DILATED_Q_BLOCK = 512
BANDED_Q_BLOCK = 256
NEIGHBOURHOOD_Q_ROWS = 2

BF16 = jnp.bfloat16
F32 = jnp.float32

Weight = collections.namedtuple("Weight", "arr prefix")


def _weight_shape(w):
    return w.arr.shape[len(w.prefix):]


def _weight_spec(w, block, index_fn):
    prefix = tuple(w.prefix)
    return pl.BlockSpec((None,) * len(prefix) + tuple(block), lambda i, j: prefix + tuple(index_fn(i, j)))


def _col_block(n, cap):
    for unit in (MXU_WIDTH, LANES):
        blocks = [c for c in range(unit, min(n, cap) + 1, unit) if n % c == 0]
        if blocks:
            return max(blocks)
    raise ValueError(f"no lane-aligned column block for {n}")


def _rms_normalize(x, g):
    ms = jnp.mean(x * x, axis=-1, keepdims=True)
    return x * lax.rsqrt(ms + RMS_EPS) * g


def _ffn_kernel(x_hbm, g_ref, wg_ref, wu_ref, wd_ref, gf_ref, o_ref, h_ref, x_buf, x_sem, *,
                n_token_blocks, n_hidden_blocks, final_norm):
    i, j = pl.program_id(0), pl.program_id(1)
    tm = x_buf.shape[0]

    def x_copy(block):
        return pltpu.make_async_copy(x_hbm.at[pl.ds(block * tm, tm), :], x_buf, x_sem)

    @pl.when((i == 0) & (j == 0))
    def _():
        x_copy(0).start()

    @pl.when(j == 0)
    def _():
        x_copy(i).wait()
        x = x_buf[...]
        h_ref[...] = _rms_normalize(x, g_ref[...]).astype(BF16)
        o_ref[...] = x

    @pl.when((j == 1) & (i + 1 < n_token_blocks))
    def _():
        x_copy(i + 1).start()

    h = h_ref[...]
    gate = jnp.dot(h, wg_ref[...].astype(BF16), preferred_element_type=F32)
    up = jnp.dot(h, wu_ref[...].astype(BF16), preferred_element_type=F32)
    act = (MACARON_WEIGHT * (gate * jax.nn.sigmoid(gate)) * up).astype(BF16)
    o_ref[...] += jnp.dot(act, wd_ref[...].astype(BF16), preferred_element_type=F32)

    if final_norm:
        @pl.when(j == n_hidden_blocks - 1)
        def _():
            o_ref[...] = _rms_normalize(o_ref[...], gf_ref[...])


def _ffn(x, g, wg, wu, wd, g_final, *, final_norm):
    t, d = x.shape
    f = _weight_shape(wg)[1]
    tm, tf = min(TOKEN_BLOCK, t), min(FFN_HIDDEN_BLOCK, f)
    assert t % tm == 0 and f % tf == 0
    ni, nj = t // tm, f // tf
    assert nj >= 2
    kern = functools.partial(_ffn_kernel, n_token_blocks=ni, n_hidden_blocks=nj,
                             final_norm=final_norm)
    return pl.pallas_call(
        kern,
        grid=(ni, nj),
        in_specs=[
            pl.BlockSpec(memory_space=pl.ANY),
            _weight_spec(g, (1, d), lambda i, j: (0, 0)),
            _weight_spec(wg, (d, tf), lambda i, j: (0, j)),
            _weight_spec(wu, (d, tf), lambda i, j: (0, j)),
            _weight_spec(wd, (tf, d), lambda i, j: (j, 0)),
            pl.BlockSpec((1, d), lambda i, j: (0, 0)),
        ],
        out_specs=pl.BlockSpec((tm, d), lambda i, j: (i, 0)),
        out_shape=jax.ShapeDtypeStruct((t, d), F32),
        scratch_shapes=[pltpu.VMEM((tm, d), BF16), pltpu.VMEM((tm, d), F32),
                        pltpu.SemaphoreType.DMA(())],
        compiler_params=pltpu.CompilerParams(
            dimension_semantics=("arbitrary", "arbitrary"),
            vmem_limit_bytes=VMEM_LIMIT_BYTES),
        name="ffn",
    )(x, g.arr, wg.arr, wu.arr, wd.arr, g_final.reshape(1, d))


def _in_proj_kernel(x_ref, g_ref, w_ref, s_ref, o_ref, h_ref):
    @pl.when(pl.program_id(1) == 0)
    def _():
        h_ref[...] = _rms_normalize(x_ref[...], g_ref[...]).astype(BF16)

    acc = jnp.dot(h_ref[...], w_ref[...].astype(BF16), preferred_element_type=F32)
    o_ref[...] = (acc * s_ref[...]).astype(o_ref.dtype)


def _in_proj(x, g, w, col_scale):
    t, d = x.shape
    n = _weight_shape(w)[1]
    tm, tn = min(TOKEN_BLOCK, t), _col_block(n, IN_PROJ_COL_BLOCK)
    assert t % tm == 0
    return pl.pallas_call(
        _in_proj_kernel,
        grid=(t // tm, n // tn),
        in_specs=[
            pl.BlockSpec((tm, d), lambda i, j: (i, 0)),
            _weight_spec(g, (1, d), lambda i, j: (0, 0)),
            _weight_spec(w, (d, tn), lambda i, j: (0, j)),
            pl.BlockSpec((1, tn), lambda i, j: (0, j)),
        ],
        out_specs=pl.BlockSpec((tm, tn), lambda i, j: (i, j)),
        out_shape=jax.ShapeDtypeStruct((t, n), BF16),
        scratch_shapes=[pltpu.VMEM((tm, d), BF16)],
        compiler_params=pltpu.CompilerParams(
            dimension_semantics=("parallel", "arbitrary"),
            vmem_limit_bytes=VMEM_LIMIT_BYTES),
        name="in_proj",
    )(x, g.arr, w.arr, col_scale.reshape(1, n))


def _out_proj_kernel(*refs, n_parts):
    o_refs, w_refs = refs[:n_parts], refs[n_parts:2 * n_parts]
    x_ref, out_ref = refs[2 * n_parts], refs[2 * n_parts + 1]
    acc = x_ref[...]
    for o_ref, w_ref in zip(o_refs, w_refs):
        acc = acc + jnp.dot(o_ref[...], w_ref[...].astype(BF16), preferred_element_type=F32)
    out_ref[...] = acc


def _out_proj(parts, w, x):
    t, d = x.shape
    tm, tn = min(TOKEN_BLOCK, t), _col_block(d, OUT_PROJ_COL_BLOCK)
    widths = [p.shape[1] for p in parts]
    assert sum(widths) == _weight_shape(w)[0] and len(set(widths)) == 1
    width = widths[0]
    o_specs = [pl.BlockSpec((tm, width), lambda i, j: (i, 0)) for _ in parts]
    w_specs = [_weight_spec(w, (width, tn), functools.partial(lambda i, j, p: (p, j), p=p))
               for p in range(len(parts))]
    return pl.pallas_call(
        functools.partial(_out_proj_kernel, n_parts=len(parts)),
        grid=(t // tm, d // tn),
        in_specs=o_specs + w_specs + [pl.BlockSpec((tm, tn), lambda i, j: (i, j))],
        out_specs=pl.BlockSpec((tm, tn), lambda i, j: (i, j)),
        out_shape=jax.ShapeDtypeStruct((t, d), F32),
        compiler_params=pltpu.CompilerParams(
            dimension_semantics=("parallel", "arbitrary"),
            vmem_limit_bytes=VMEM_LIMIT_BYTES),
        name="out_proj",
    )(*parts, *([w.arr] * len(parts)), x)


def _attn_kernel(*refs, tiles, use_sink, plans):
    refs = list(refs)
    q_ref, k_ref, v_ref, table_ref = refs[:4]
    sink_ref = refs[4] if use_sink else None
    o_ref = refs[4 + use_sink]
    if use_sink:
        sink = sink_ref[0, :, 0:1]

    if plans is None:
        bias_tile = lambda case, c0, kw: table_ref[0, case, :, c0:c0 + kw]
    else:
        bias_ref = refs[5 + use_sink]
        w = GRID_W

        @pl.when(pl.program_id(1) == 0)
        def _():
            for case, plan in enumerate(plans):
                for qr, row in enumerate(plan):
                    for kp, block in enumerate(row):
                        bias_ref[case, qr * w:(qr + 1) * w, kp * 2 * w:(kp + 1) * 2 * w] = \
                            table_ref[0, block]

        bias_tile = lambda case, c0, kw: bias_ref[case, :, c0:c0 + kw]

    dh = v_ref.shape[1]
    vx_ref = refs[-1]
    vx_ref[:, :dh] = v_ref[...]
    vx_ref[:, dh:] = jnp.ones((v_ref.shape[0], dh), BF16)
    for q0, bq, k0, kw, case, c0 in tiles:
        q = q_ref[q0:q0 + bq, :]
        k = k_ref[k0:k0 + kw, :]
        s = lax.dot_general(q, k, (((1,), (1,)), ((), ())), preferred_element_type=F32)
        s = s + bias_tile(case, c0, kw)
        m = jnp.max(s, axis=-1, keepdims=True)
        if use_sink:
            m = jnp.maximum(m, sink)
        p = jnp.exp2(s - m)
        o = jnp.dot(p.astype(BF16), vx_ref[k0:k0 + kw, :], preferred_element_type=F32)
        denom = o[:, dh:]
        if use_sink:
            denom = denom + jnp.exp2(sink - m)
        o_ref[q0:q0 + bq, :] = (o[:, :dh] / denom).astype(o_ref.dtype)


def _attention(qkv, table, sink, *, batch, seq, n_heads, group, q_col, k_col, v_col, tiles,
               plans=None):
    dh = HEAD_DIM
    use_sink = sink is not None
    in_specs = [
        pl.BlockSpec((seq, dh), lambda h, b: (b, q_col + h)),
        pl.BlockSpec((seq, dh), lambda h, b: (b, k_col + h // group)),
        pl.BlockSpec((seq, dh), lambda h, b: (b, v_col + h // group)),
        pl.BlockSpec((1,) + table.shape[1:], lambda h, b: (h, 0, 0, 0)),
    ]
    args = [qkv, qkv, qkv, table]
    if use_sink:
        in_specs.append(pl.BlockSpec((1, 1, dh), lambda h, b: (h, 0, 0)))
        args.append(jnp.broadcast_to(LOG2_E * sink.astype(F32)[:, None, None], (n_heads, 1, dh)))
    scratch = []
    if plans is not None:
        q_rows, k_pairs = len(plans[0]), len(plans[0][0])
        scratch.append(pltpu.VMEM((len(plans), q_rows * GRID_W, k_pairs * 2 * GRID_W), F32))
    scratch.append(pltpu.VMEM((seq, 2 * dh), BF16))
    return pl.pallas_call(
        functools.partial(_attn_kernel, tiles=tuple(tiles), use_sink=use_sink, plans=plans),
        grid=(n_heads, batch),
        in_specs=in_specs,
        out_specs=pl.BlockSpec((seq, dh), lambda h, b: (b, h)),
        out_shape=jax.ShapeDtypeStruct((batch * seq, n_heads * dh), BF16),
        scratch_shapes=scratch,
        compiler_params=pltpu.CompilerParams(
            dimension_semantics=("parallel", "arbitrary"),
            vmem_limit_bytes=VMEM_LIMIT_BYTES),
        name="attention",
    )(*args)


def _alibi_slopes(n):
    return 2.0 ** (-8.0 * jnp.arange(1, n + 1, dtype=F32) / n)


def _toeplitz_tiles(seq, bq, halo):
    tiles = []
    for q0 in range(0, seq, bq):
        k0, k1 = max(0, q0 - halo), min(seq, q0 + bq + halo)
        tiles.append((q0, bq, k0, k1 - k0, 0, k0 - q0 + halo))
    return tiles


def _toeplitz_offsets(bq, halo):
    return (jnp.arange(bq + 2 * halo, dtype=jnp.int32)[None, :] - halo
            - jnp.arange(bq, dtype=jnp.int32)[:, None])


def _dilated_bias(bq, halo):
    d = _toeplitz_offsets(bq, halo)
    ad = jnp.abs(d)
    count = jnp.zeros(d.shape, F32)
    for window, r in A_PATTERNS:
        count = count + ((d % r == 0) & (ad <= (window // (2 * r)) * r)).astype(F32)
    log_count = jnp.where(count > 0, jnp.log(jnp.maximum(count, 1.0)), NEG_INF)
    slopes = _alibi_slopes(A_HEADS)
    return (LOG2_E * (log_count[None] - slopes[:, None, None] * ad.astype(F32)[None]))[:, None]


def _banded_bias(bq, halo):
    d = _toeplitz_offsets(bq, halo)
    ad = jnp.abs(d)
    slopes = _alibi_slopes(B_HEADS)
    bias = jnp.where((ad <= B_HALF_WINDOW)[None], -slopes[:, None, None] * ad.astype(F32)[None], NEG_INF)
    return LOG2_E * bias[:, None]


def _neighbourhood_geometry(seq, q_rows, k_rows):
    rows = seq // GRID_W
    kr = min(C_KR_MAX, rows)
    k_rows = min(k_rows, rows)
    assert rows % q_rows == 0 and k_rows % 2 == 0 and k_rows >= min(rows, q_rows + kr - 1)
    n_r, n_c = 2 * C_KR_MAX - 1, 2 * C_KC - 1
    qc, kc = np.arange(GRID_W)[:, None], np.arange(GRID_W)[None, :]
    cs = np.clip(qc - C_KC // 2, 0, GRID_W - C_KC)
    in_c = (kc >= cs) & (kc < cs + C_KC)
    ci = np.where(in_c, np.clip(kc - qc + C_KC - 1, 0, n_c - 1), n_c)
    col_sel = (ci[..., None] == np.arange(n_c + 1)).astype(np.float32)
    blocks, cases, tiles = {}, {}, []
    for r0 in range(0, rows, q_rows):
        kstart = int(np.clip(r0 - kr // 2, 0, rows - k_rows))
        qr = r0 + np.arange(q_rows)[:, None]
        krow = kstart + np.arange(k_rows)[None, :]
        rs = np.clip(qr - kr // 2, 0, rows - kr)
        in_r = (krow >= rs) & (krow < rs + kr)
        ri = np.where(in_r, np.clip(krow - qr + C_KR_MAX - 1, 0, n_r - 1), n_r)
        plan = tuple(tuple(blocks.setdefault((int(ri[q, 2 * p]), int(ri[q, 2 * p + 1])), len(blocks))
                           for p in range(k_rows // 2)) for q in range(q_rows))
        case = cases.setdefault(plan, len(cases))
        tiles.append((r0 * GRID_W, q_rows * GRID_W, kstart * GRID_W, k_rows * GRID_W, case, 0))
    row_pair_sel = np.zeros((len(blocks), 2, n_r + 1), np.float32)
    for (i0, i1), b in blocks.items():
        row_pair_sel[b, 0, i0] = 1.0
        row_pair_sel[b, 1, i1] = 1.0
    plans = tuple(sorted(cases, key=cases.get))
    col_pair_sel = np.einsum('ts,xyj->tjxsy', np.eye(2, dtype=np.float32), col_sel)
    col_pair_sel = col_pair_sel.reshape(2 * (n_c + 1), GRID_W * 2 * GRID_W)
    return tiles, plans, (row_pair_sel, col_pair_sel)


def _neighbourhood_blocks(rpb, selectors):
    row_pair_sel, col_pair_sel = selectors
    hp = lax.Precision.HIGHEST
    table = jnp.pad(LOG2_E * rpb.astype(F32), ((0, 0), (0, 1), (0, 1)), constant_values=NEG_INF)
    t = jnp.einsum('hij,psi->hpsj', table, row_pair_sel, precision=hp)
    n_heads, n_blocks = t.shape[:2]
    t = jnp.dot(t.reshape(n_heads * n_blocks, -1), col_pair_sel, precision=hp)
    return t.reshape(n_heads, n_blocks, GRID_W, 2 * GRID_W)


def kernel(x, ffn_norm, ffn_w_gate, ffn_w_up, ffn_w_down, mix_norm, ab_w_in, ab_w_out, ab_sink,
           c_w_in, c_w_out, c_rpb, final_norm):
    batch, seq, d = x.shape
    depth = ffn_norm.shape[0]
    dh = HEAD_DIM
    scale = LOG2_E * dh ** -0.5

    da, db, dkv = A_HEADS * dh, B_HEADS * dh, B_KV_HEADS * dh
    ab_scale = jnp.concatenate([jnp.full((da,), scale, F32), jnp.ones((2 * da,), F32),
                                jnp.full((db,), scale, F32), jnp.ones((2 * dkv,), F32)])
    dc = C_HEADS * dh
    c_scale = jnp.concatenate([jnp.full((dc,), scale, F32), jnp.ones((2 * dc,), F32)])

    bq_a, bq_b = min(DILATED_Q_BLOCK, seq), min(BANDED_Q_BLOCK, seq)
    halo_a = max((w // (2 * r)) * r for w, r in A_PATTERNS)
    tiles_a, bias_a = _toeplitz_tiles(seq, bq_a, halo_a), _dilated_bias(bq_a, halo_a)
    tiles_b, bias_b = _toeplitz_tiles(seq, bq_b, B_HALF_WINDOW), _banded_bias(bq_b, B_HALF_WINDOW)
    rows = seq // GRID_W
    q_rows = math.gcd(rows, NEIGHBOURHOOD_Q_ROWS)
    tiles_c, plans_c, selectors_c = _neighbourhood_geometry(seq, q_rows, q_rows + C_KR_MAX)

    ffn_gain = ffn_norm.reshape(depth, 2, 1, d)
    mix_gain = mix_norm.reshape(depth, 1, d)

    def ffn_sources(layer, half):
        return [Weight(w, (layer, half)) for w in (ffn_w_gate, ffn_w_up, ffn_w_down)]

    xf = x.reshape(batch * seq, d)
    for layer in range(depth):
        i = layer // 2
        w_in, w_out = ((Weight(ab_w_in, (i,)), Weight(ab_w_out, (i,))) if layer % 2 == 0
                       else (Weight(c_w_in, (i,)), Weight(c_w_out, (i,))))
        xf = _ffn(xf, Weight(ffn_gain, (layer, 0)), *ffn_sources(layer, 0), final_norm,
                  final_norm=False)
        gain = Weight(mix_gain, (layer,))
        if layer % 2 == 0:
            proj = _in_proj(xf, gain, w_in, ab_scale)
            oa = _attention(proj, bias_a, None, batch=batch, seq=seq, n_heads=A_HEADS, group=1,
                            q_col=0, k_col=A_HEADS, v_col=2 * A_HEADS, tiles=tiles_a)
            ob = _attention(proj, bias_b, ab_sink[i], batch=batch, seq=seq, n_heads=B_HEADS,
                            group=B_HEADS // B_KV_HEADS, q_col=3 * A_HEADS,
                            k_col=3 * A_HEADS + B_HEADS, v_col=3 * A_HEADS + B_HEADS + B_KV_HEADS,
                            tiles=tiles_b)
            xf = _out_proj([oa, ob], w_out, xf)
        else:
            proj = _in_proj(xf, gain, w_in, c_scale)
            oc = _attention(proj, _neighbourhood_blocks(c_rpb[i], selectors_c), None, batch=batch,
                            seq=seq, n_heads=C_HEADS, group=1, q_col=0, k_col=C_HEADS,
                            v_col=2 * C_HEADS, tiles=tiles_c, plans=plans_c)
            xf = _out_proj([oc], w_out, xf)
        xf = _ffn(xf, Weight(ffn_gain, (layer, 1)), *ffn_sources(layer, 1), final_norm,
                  final_norm=(layer == depth - 1))
    return xf.reshape(batch, seq, d)
```

```python
import collections
import functools
import math

import numpy as np
import jax
import jax.numpy as jnp
from jax import lax
from jax.experimental import pallas as pl
from jax.experimental.pallas import tpu as pltpu

HEAD_DIM = 128
MACARON_WEIGHT = 0.5
RMS_EPS = 1e-6
NEG_INF = -1e30
LOG2_E = math.log2(math.e)

A_HEADS = 8
A_PATTERNS = ((128, 1), (512, 4), (2048, 16))
B_HEADS = 8
B_KV_HEADS = 2
B_HALF_WINDOW = 128
C_HEADS = 16
GRID_W = 64
C_KR_MAX = 8
C_KC = 16

LANES = 128
MXU_WIDTH = 256
VMEM_LIMIT_BYTES = 60 * 1024 * 1024

TOKEN_BLOCK = 1024
FFN_HIDDEN_BLOCK = 512
IN_PROJ_COL_BLOCK = 1536
OUT_PROJ_COL_BLOCK = 1024
DILATED_Q_BLOCK = 512
BANDED_Q_BLOCK = 256
NEIGHBOURHOOD_Q_ROWS = 2

BF16 = jnp.bfloat16
F32 = jnp.float32

Weight = collections.namedtuple("Weight", "arr prefix")


def _weight_shape(w):
    return w.arr.shape[len(w.prefix):]


def _weight_spec(w, block, index_fn):
    prefix = tuple(w.prefix)
    return pl.BlockSpec((None,) * len(prefix) + tuple(block), lambda i, j: prefix + tuple(index_fn(i, j)))


def _col_block(n, cap):
    for unit in (MXU_WIDTH, LANES):
        blocks = [c for c in range(unit, min(n, cap) + 1, unit) if n % c == 0]
        if blocks:
            return max(blocks)
    raise ValueError(f"no lane-aligned column block for {n}")


def _cast_specs(casts, ni, nj):
    in_specs, out_specs, out_shapes = [], [], []
    for w in casts:
        r, c = _weight_shape(w)
        assert r % (16 * ni) == 0
        rb = r // ni
        cb = min(cb for cb in range(LANES, c + 1, LANES) if c % cb == 0 and c // cb <= nj)
        last = c // cb - 1
        index_fn = functools.partial(lambda i, j, last: (i, jnp.minimum(j, last)), last=last)
        in_specs.append(_weight_spec(w, (rb, cb), index_fn))
        out_specs.append(pl.BlockSpec((rb, cb), index_fn))
        out_shapes.append(jax.ShapeDtypeStruct((r, c), BF16))
    return in_specs, out_specs, out_shapes


def _rms_normalize(x, g):
    ms = jnp.mean(x * x, axis=-1, keepdims=True)
    return x * lax.rsqrt(ms + RMS_EPS) * g


def _ffn_kernel(x_hbm, g_ref, wg_ref, wu_ref, wd_ref, gf_ref, o_ref, h_ref, x_buf, x_sem, *,
                n_token_blocks, n_hidden_blocks, final_norm):
    i, j = pl.program_id(0), pl.program_id(1)
    tm = x_buf.shape[0]

    def x_copy(block):
        return pltpu.make_async_copy(x_hbm.at[pl.ds(block * tm, tm), :], x_buf, x_sem)

    @pl.when((i == 0) & (j == 0))
    def _():
        x_copy(0).start()

    @pl.when(j == 0)
    def _():
        x_copy(i).wait()
        x = x_buf[...]
        h_ref[...] = _rms_normalize(x, g_ref[...]).astype(BF16)
        o_ref[...] = x

    @pl.when((j == 1) & (i + 1 < n_token_blocks))
    def _():
        x_copy(i + 1).start()

    h = h_ref[...]
    gate = jnp.dot(h, wg_ref[...].astype(BF16), preferred_element_type=F32)
    up = jnp.dot(h, wu_ref[...].astype(BF16), preferred_element_type=F32)
    act = (MACARON_WEIGHT * (gate * jax.nn.sigmoid(gate)) * up).astype(BF16)
    o_ref[...] += jnp.dot(act, wd_ref[...].astype(BF16), preferred_element_type=F32)

    if final_norm:
        @pl.when(j == n_hidden_blocks - 1)
        def _():
            o_ref[...] = _rms_normalize(o_ref[...], gf_ref[...])


def _ffn(x, g, wg, wu, wd, g_final, *, final_norm):
    t, d = x.shape
    f = _weight_shape(wg)[1]
    tm, tf = min(TOKEN_BLOCK, t), min(FFN_HIDDEN_BLOCK, f)
    assert t % tm == 0 and f % tf == 0
    ni, nj = t // tm, f // tf
    assert nj >= 2
    kern = functools.partial(_ffn_kernel, n_token_blocks=ni, n_hidden_blocks=nj,
                             final_norm=final_norm)
    return pl.pallas_call(
        kern,
        grid=(ni, nj),
        in_specs=[
            pl.BlockSpec(memory_space=pl.ANY),
            _weight_spec(g, (1, d), lambda i, j: (0, 0)),
            _weight_spec(wg, (d, tf), lambda i, j: (0, j)),
            _weight_spec(wu, (d, tf), lambda i, j: (0, j)),
            _weight_spec(wd, (tf, d), lambda i, j: (j, 0)),
            pl.BlockSpec((1, d), lambda i, j: (0, 0)),
        ],
        out_specs=pl.BlockSpec((tm, d), lambda i, j: (i, 0)),
        out_shape=jax.ShapeDtypeStruct((t, d), F32),
        scratch_shapes=[pltpu.VMEM((tm, d), BF16), pltpu.VMEM((tm, d), F32),
                        pltpu.SemaphoreType.DMA(())],
        compiler_params=pltpu.CompilerParams(
            dimension_semantics=("arbitrary", "arbitrary"),
            vmem_limit_bytes=VMEM_LIMIT_BYTES),
        name="ffn",
    )(x, g.arr, wg.arr, wu.arr, wd.arr, g_final.reshape(1, d))


def _in_proj_kernel(*refs, n_casts):
    x_ref, g_ref, w_ref, s_ref = refs[:4]
    src_refs = refs[4:4 + n_casts]
    o_ref = refs[4 + n_casts]
    dst_refs = refs[5 + n_casts:5 + 2 * n_casts]
    h_ref = refs[5 + 2 * n_casts]

    @pl.when(pl.program_id(1) == 0)
    def _():
        h_ref[...] = _rms_normalize(x_ref[...], g_ref[...]).astype(BF16)

    for src_ref, dst_ref in zip(src_refs, dst_refs):
        dst_ref[...] = src_ref[...].astype(BF16)

    acc = jnp.dot(h_ref[...], w_ref[...].astype(BF16), preferred_element_type=F32)
    o_ref[...] = (acc * s_ref[...]).astype(o_ref.dtype)


def _in_proj(x, g, w, col_scale, casts):
    t, d = x.shape
    n = _weight_shape(w)[1]
    tm, tn = min(TOKEN_BLOCK, t), _col_block(n, IN_PROJ_COL_BLOCK)
    assert t % tm == 0
    ni, nj = t // tm, n // tn
    cast_in, cast_out, cast_shapes = _cast_specs(casts, ni, nj)
    outs = pl.pallas_call(
        functools.partial(_in_proj_kernel, n_casts=len(casts)),
        grid=(ni, nj),
        in_specs=[
            pl.BlockSpec((tm, d), lambda i, j: (i, 0)),
            _weight_spec(g, (1, d), lambda i, j: (0, 0)),
            _weight_spec(w, (d, tn), lambda i, j: (0, j)),
            pl.BlockSpec((1, tn), lambda i, j: (0, j)),
        ] + cast_in,
        out_specs=[pl.BlockSpec((tm, tn), lambda i, j: (i, j))] + cast_out,
        out_shape=[jax.ShapeDtypeStruct((t, n), BF16)] + cast_shapes,
        scratch_shapes=[pltpu.VMEM((tm, d), BF16)],
        compiler_params=pltpu.CompilerParams(
            dimension_semantics=("parallel", "arbitrary"),
            vmem_limit_bytes=VMEM_LIMIT_BYTES),
        name="in_proj",
    )(x, g.arr, w.arr, col_scale.reshape(1, n), *[c.arr for c in casts])
    return outs[0], list(outs[1:])


def _out_proj_kernel(*refs, n_parts):
    o_refs, w_refs = refs[:n_parts], refs[n_parts:2 * n_parts]
    x_ref, out_ref = refs[2 * n_parts], refs[2 * n_parts + 1]
    acc = x_ref[...]
    for o_ref, w_ref in zip(o_refs, w_refs):
        acc = acc + jnp.dot(o_ref[...], w_ref[...].astype(BF16), preferred_element_type=F32)
    out_ref[...] = acc


def _out_proj(parts, w, x):
    t, d = x.shape
    tm, tn = min(TOKEN_BLOCK, t), _col_block(d, OUT_PROJ_COL_BLOCK)
    widths = [p.shape[1] for p in parts]
    assert sum(widths) == _weight_shape(w)[0] and len(set(widths)) == 1
    width = widths[0]
    o_specs = [pl.BlockSpec((tm, width), lambda i, j: (i, 0)) for _ in parts]
    w_specs = [_weight_spec(w, (width, tn), functools.partial(lambda i, j, p: (p, j), p=p))
               for p in range(len(parts))]
    return pl.pallas_call(
        functools.partial(_out_proj_kernel, n_parts=len(parts)),
        grid=(t // tm, d // tn),
        in_specs=o_specs + w_specs + [pl.BlockSpec((tm, tn), lambda i, j: (i, j))],
        out_specs=pl.BlockSpec((tm, tn), lambda i, j: (i, j)),
        out_shape=jax.ShapeDtypeStruct((t, d), F32),
        compiler_params=pltpu.CompilerParams(
            dimension_semantics=("parallel", "arbitrary"),
            vmem_limit_bytes=VMEM_LIMIT_BYTES),
        name="out_proj",
    )(*parts, *([w.arr] * len(parts)), x)


def _attn_kernel(*refs, tiles, use_sink, plans):
    refs = list(refs)
    q_ref, k_ref, v_ref, table_ref = refs[:4]
    sink_ref = refs[4] if use_sink else None
    o_ref = refs[4 + use_sink]
    if use_sink:
        sink = sink_ref[0, :, 0:1]

    if plans is None:
        bias_tile = lambda case, c0, kw: table_ref[0, case, :, c0:c0 + kw]
    else:
        bias_ref = refs[5 + use_sink]
        w = GRID_W

        @pl.when(pl.program_id(1) == 0)
        def _():
            for case, plan in enumerate(plans):
                for qr, row in enumerate(plan):
                    for kp, block in enumerate(row):
                        bias_ref[case, qr * w:(qr + 1) * w, kp * 2 * w:(kp + 1) * 2 * w] = \
                            table_ref[0, block]

        bias_tile = lambda case, c0, kw: bias_ref[case, :, c0:c0 + kw]

    dh = v_ref.shape[1]
    vx_ref = refs[-1]
    vx_ref[:, :dh] = v_ref[...]
    vx_ref[:, dh:] = jnp.ones((v_ref.shape[0], dh), BF16)
    for q0, bq, k0, kw, case, c0 in tiles:
        q = q_ref[q0:q0 + bq, :]
        k = k_ref[k0:k0 + kw, :]
        s = lax.dot_general(q, k, (((1,), (1,)), ((), ())), preferred_element_type=F32)
        s = s + bias_tile(case, c0, kw)
        m = jnp.max(s, axis=-1, keepdims=True)
        if use_sink:
            m = jnp.maximum(m, sink)
        p = jnp.exp2(s - m)
        o = jnp.dot(p.astype(BF16), vx_ref[k0:k0 + kw, :], preferred_element_type=F32)
        denom = o[:, dh:]
        if use_sink:
            denom = denom + jnp.exp2(sink - m)
        o_ref[q0:q0 + bq, :] = (o[:, :dh] / denom).astype(o_ref.dtype)


def _attention(qkv, table, sink, *, batch, seq, n_heads, group, q_col, k_col, v_col, tiles,
               plans=None):
    dh = HEAD_DIM
    use_sink = sink is not None
    in_specs = [
        pl.BlockSpec((seq, dh), lambda h, b: (b, q_col + h)),
        pl.BlockSpec((seq, dh), lambda h, b: (b, k_col + h // group)),
        pl.BlockSpec((seq, dh), lambda h, b: (b, v_col + h // group)),
        pl.BlockSpec((1,) + table.shape[1:], lambda h, b: (h, 0, 0, 0)),
    ]
    args = [qkv, qkv, qkv, table]
    if use_sink:
        in_specs.append(pl.BlockSpec((1, 1, dh), lambda h, b: (h, 0, 0)))
        args.append(jnp.broadcast_to(LOG2_E * sink.astype(F32)[:, None, None], (n_heads, 1, dh)))
    scratch = []
    if plans is not None:
        q_rows, k_pairs = len(plans[0]), len(plans[0][0])
        scratch.append(pltpu.VMEM((len(plans), q_rows * GRID_W, k_pairs * 2 * GRID_W), F32))
    scratch.append(pltpu.VMEM((seq, 2 * dh), BF16))
    return pl.pallas_call(
        functools.partial(_attn_kernel, tiles=tuple(tiles), use_sink=use_sink, plans=plans),
        grid=(n_heads, batch),
        in_specs=in_specs,
        out_specs=pl.BlockSpec((seq, dh), lambda h, b: (b, h)),
        out_shape=jax.ShapeDtypeStruct((batch * seq, n_heads * dh), BF16),
        scratch_shapes=scratch,
        compiler_params=pltpu.CompilerParams(
            dimension_semantics=("parallel", "arbitrary"),
            vmem_limit_bytes=VMEM_LIMIT_BYTES),
        name="attention",
    )(*args)


def _alibi_slopes(n):
    return 2.0 ** (-8.0 * jnp.arange(1, n + 1, dtype=F32) / n)


def _toeplitz_tiles(seq, bq, halo):
    tiles = []
    for q0 in range(0, seq, bq):
        k0, k1 = max(0, q0 - halo), min(seq, q0 + bq + halo)
        tiles.append((q0, bq, k0, k1 - k0, 0, k0 - q0 + halo))
    return tiles


def _toeplitz_offsets(bq, halo):
    return (jnp.arange(bq + 2 * halo, dtype=jnp.int32)[None, :] - halo
            - jnp.arange(bq, dtype=jnp.int32)[:, None])


def _dilated_bias(bq, halo):
    d = _toeplitz_offsets(bq, halo)
    ad = jnp.abs(d)
    count = jnp.zeros(d.shape, F32)
    for window, r in A_PATTERNS:
        count = count + ((d % r == 0) & (ad <= (window // (2 * r)) * r)).astype(F32)
    log_count = jnp.where(count > 0, jnp.log(jnp.maximum(count, 1.0)), NEG_INF)
    slopes = _alibi_slopes(A_HEADS)
    return (LOG2_E * (log_count[None] - slopes[:, None, None] * ad.astype(F32)[None]))[:, None]


def _banded_bias(bq, halo):
    d = _toeplitz_offsets(bq, halo)
    ad = jnp.abs(d)
    slopes = _alibi_slopes(B_HEADS)
    bias = jnp.where((ad <= B_HALF_WINDOW)[None], -slopes[:, None, None] * ad.astype(F32)[None], NEG_INF)
    return LOG2_E * bias[:, None]


def _neighbourhood_geometry(seq, q_rows, k_rows):
    rows = seq // GRID_W
    kr = min(C_KR_MAX, rows)
    k_rows = min(k_rows, rows)
    assert rows % q_rows == 0 and k_rows % 2 == 0 and k_rows >= min(rows, q_rows + kr - 1)
    n_r, n_c = 2 * C_KR_MAX - 1, 2 * C_KC - 1
    qc, kc = np.arange(GRID_W)[:, None], np.arange(GRID_W)[None, :]
    cs = np.clip(qc - C_KC // 2, 0, GRID_W - C_KC)
    in_c = (kc >= cs) & (kc < cs + C_KC)
    ci = np.where(in_c, np.clip(kc - qc + C_KC - 1, 0, n_c - 1), n_c)
    col_sel = (ci[..., None] == np.arange(n_c + 1)).astype(np.float32)
    blocks, cases, tiles = {}, {}, []
    for r0 in range(0, rows, q_rows):
        kstart = int(np.clip(r0 - kr // 2, 0, rows - k_rows))
        qr = r0 + np.arange(q_rows)[:, None]
        krow = kstart + np.arange(k_rows)[None, :]
        rs = np.clip(qr - kr // 2, 0, rows - kr)
        in_r = (krow >= rs) & (krow < rs + kr)
        ri = np.where(in_r, np.clip(krow - qr + C_KR_MAX - 1, 0, n_r - 1), n_r)
        plan = tuple(tuple(blocks.setdefault((int(ri[q, 2 * p]), int(ri[q, 2 * p + 1])), len(blocks))
                           for p in range(k_rows // 2)) for q in range(q_rows))
        case = cases.setdefault(plan, len(cases))
        tiles.append((r0 * GRID_W, q_rows * GRID_W, kstart * GRID_W, k_rows * GRID_W, case, 0))
    row_pair_sel = np.zeros((len(blocks), 2, n_r + 1), np.float32)
    for (i0, i1), b in blocks.items():
        row_pair_sel[b, 0, i0] = 1.0
        row_pair_sel[b, 1, i1] = 1.0
    plans = tuple(sorted(cases, key=cases.get))
    col_pair_sel = np.einsum('ts,xyj->tjxsy', np.eye(2, dtype=np.float32), col_sel)
    col_pair_sel = col_pair_sel.reshape(2 * (n_c + 1), GRID_W * 2 * GRID_W)
    return tiles, plans, (row_pair_sel, col_pair_sel)


def _neighbourhood_blocks(rpb, selectors):
    row_pair_sel, col_pair_sel = selectors
    hp = lax.Precision.HIGHEST
    table = jnp.pad(LOG2_E * rpb.astype(F32), ((0, 0), (0, 1), (0, 1)), constant_values=NEG_INF)
    t = jnp.einsum('hij,psi->hpsj', table, row_pair_sel, precision=hp)
    n_heads, n_blocks = t.shape[:2]
    t = jnp.dot(t.reshape(n_heads * n_blocks, -1), col_pair_sel, precision=hp)
    return t.reshape(n_heads, n_blocks, GRID_W, 2 * GRID_W)


def kernel(x, ffn_norm, ffn_w_gate, ffn_w_up, ffn_w_down, mix_norm, ab_w_in, ab_w_out, ab_sink,
           c_w_in, c_w_out, c_rpb, final_norm):
    batch, seq, d = x.shape
    depth = ffn_norm.shape[0]
    dh = HEAD_DIM
    scale = LOG2_E * dh ** -0.5

    da, db, dkv = A_HEADS * dh, B_HEADS * dh, B_KV_HEADS * dh
    ab_scale = jnp.concatenate([jnp.full((da,), scale, F32), jnp.ones((2 * da,), F32),
                                jnp.full((db,), scale, F32), jnp.ones((2 * dkv,), F32)])
    dc = C_HEADS * dh
    c_scale = jnp.concatenate([jnp.full((dc,), scale, F32), jnp.ones((2 * dc,), F32)])

    bq_a, bq_b = min(DILATED_Q_BLOCK, seq), min(BANDED_Q_BLOCK, seq)
    halo_a = max((w // (2 * r)) * r for w, r in A_PATTERNS)
    tiles_a, bias_a = _toeplitz_tiles(seq, bq_a, halo_a), _dilated_bias(bq_a, halo_a)
    tiles_b, bias_b = _toeplitz_tiles(seq, bq_b, B_HALF_WINDOW), _banded_bias(bq_b, B_HALF_WINDOW)
    rows = seq // GRID_W
    q_rows = math.gcd(rows, NEIGHBOURHOOD_Q_ROWS)
    tiles_c, plans_c, selectors_c = _neighbourhood_geometry(seq, q_rows, q_rows + C_KR_MAX)

    ffn_gain = ffn_norm.reshape(depth, 2, 1, d)
    mix_gain = mix_norm.reshape(depth, 1, d)

    def ffn_sources(layer, half):
        return [Weight(w, (layer, half)) for w in (ffn_w_gate, ffn_w_up, ffn_w_down)]

    def mixer_sources(layer):
        i = layer // 2
        return ((Weight(ab_w_in, (i,)), Weight(ab_w_out, (i,))) if layer % 2 == 0
                else (Weight(c_w_in, (i,)), Weight(c_w_out, (i,))))

    w_in_source = mixer_sources(0)[0]
    w_in = Weight(w_in_source.arr[w_in_source.prefix].astype(BF16), ())

    xf = x.reshape(batch * seq, d)
    for layer in range(depth):
        i = layer // 2
        xf = _ffn(xf, Weight(ffn_gain, (layer, 0)), *ffn_sources(layer, 0), final_norm,
                  final_norm=False)
        gain = Weight(mix_gain, (layer,))
        casts = [mixer_sources(layer)[1]] + ([mixer_sources(layer + 1)[0]] if layer + 1 < depth else [])
        proj, cast = _in_proj(xf, gain, w_in, ab_scale if layer % 2 == 0 else c_scale, casts)
        w_out = Weight(cast[0], ())
        if layer + 1 < depth:
            w_in = Weight(cast[1], ())
        if layer % 2 == 0:
            oa = _attention(proj, bias_a, None, batch=batch, seq=seq, n_heads=A_HEADS, group=1,
                            q_col=0, k_col=A_HEADS, v_col=2 * A_HEADS, tiles=tiles_a)
            ob = _attention(proj, bias_b, ab_sink[i], batch=batch, seq=seq, n_heads=B_HEADS,
                            group=B_HEADS // B_KV_HEADS, q_col=3 * A_HEADS,
                            k_col=3 * A_HEADS + B_HEADS, v_col=3 * A_HEADS + B_HEADS + B_KV_HEADS,
                            tiles=tiles_b)
            xf = _out_proj([oa, ob], w_out, xf)
        else:
            oc = _attention(proj, _neighbourhood_blocks(c_rpb[i], selectors_c), None, batch=batch,
                            seq=seq, n_heads=C_HEADS, group=1, q_col=0, k_col=C_HEADS,
                            v_col=2 * C_HEADS, tiles=tiles_c, plans=plans_c)
            xf = _out_proj([oc], w_out, xf)
        xf = _ffn(xf, Weight(ffn_gain, (layer, 1)), *ffn_sources(layer, 1), final_norm,
                  final_norm=(layer == depth - 1))
    return xf.reshape(batch, seq, d)
```
